```python
import jax, jax.numpy as jnp
from jax import lax
import numpy as np

D_MODEL = 1024
BATCH = 8
SEQ = 4096
DEPTH = 1
DEC_BATCH = 2
DEC_SEQ = 16384
PAST_LEN = 128

N_MEM = 256
D_CONV = 512
N_CONV_GROUPS = 8
CONV_WIDTH = 3
D_GMLP = 512
N_GMLP_HEADS = 8
GMLP_HEAD_DIM = D_GMLP // N_GMLP_HEADS
CHUNK = 128
D_IN = 3 * D_CONV + 2 * D_GMLP
D_MIX = D_CONV + D_GMLP
D_FF = 2816
N_XHEADS = 4
XHEAD_DIM = D_MODEL // N_XHEADS
EPS = 1e-6

kernel_name = "hybrid_conv_gmlp_macaron_encoder"


def rms_norm(x, g):
    xf = x.astype(jnp.float32)
    y = xf * lax.rsqrt(jnp.mean(xf * xf, axis=-1, keepdims=True) + EPS)
    return (y * g.astype(jnp.float32)).astype(x.dtype)


def layer_norm(x, g, b):
    xf = x.astype(jnp.float32)
    mu = jnp.mean(xf, axis=-1, keepdims=True)
    xc = xf - mu
    y = xc * lax.rsqrt(jnp.mean(xc * xc, axis=-1, keepdims=True) + EPS)
    return (y * g.astype(jnp.float32) + b.astype(jnp.float32)).astype(x.dtype)


def swiglu(x, w_gu, w_down):
    g, u = jnp.split(x @ w_gu, 2, axis=-1)
    return (jax.nn.silu(g) * u) @ w_down


def short_conv(z, w):
    zp = jnp.pad(z, ((0, 0), (1, 1), (0, 0)))
    return w[0] * zp[:, :-2] + w[1] * zp[:, 1:-1] + w[2] * zp[:, 2:]


def spatial_gate(u, v, w_s, b_s, ln_g, ln_b):
    bsz, s, _ = v.shape
    v = layer_norm(v, ln_g, ln_b)
    vc = v.reshape(bsz, s // CHUNK, CHUNK, N_GMLP_HEADS, GMLP_HEAD_DIM)
    mixed = jnp.einsum('hpq,bcqhd->bcphd', w_s, vc) + b_s.T[None, None, :, :, None]
    return u * mixed.reshape(bsz, s, D_GMLP)


def cross_attention(h, mem_n, w_q, w_kv, w_o):
    bsz, s, _ = h.shape
    q = (h @ w_q).reshape(bsz, s, N_XHEADS, XHEAD_DIM)
    k, v = jnp.split(mem_n @ w_kv, 2, axis=-1)
    k = k.reshape(bsz, N_MEM, N_XHEADS, XHEAD_DIM)
    v = v.reshape(bsz, N_MEM, N_XHEADS, XHEAD_DIM)
    scores = jnp.einsum('bshd,bmhd->bhsm', q, k).astype(jnp.float32) * (XHEAD_DIM ** -0.5)
    attn = jax.nn.softmax(scores, axis=-1).astype(v.dtype)
    o = jnp.einsum('bhsm,bmhd->bshd', attn, v).reshape(bsz, s, D_MODEL)
    return o @ w_o


def encoder_layer(x, mem, p, i):
    x = x + 0.5 * swiglu(rms_norm(x, p['norm_ffn1'][i]), p['ffn1_w_gu'][i], p['ffn1_w_down'][i])
    n = rms_norm(x, p['norm_mix'][i])
    proj = n @ p['w_in'][i]
    b_gate, c_gate, x_in, u, v = jnp.split(
        proj, [D_CONV, 2 * D_CONV, 3 * D_CONV, 3 * D_CONV + D_GMLP], axis=-1)
    y_conv = b_gate * short_conv(c_gate * x_in, p['conv_w'][i])
    y_gmlp = spatial_gate(jax.nn.gelu(u), jax.nn.gelu(v), p['gmlp_w_s'][i], p['gmlp_b_s'][i],
                          p['gmlp_ln_g'][i], p['gmlp_ln_b'][i])
    y_mix = jnp.concatenate([rms_norm(y_conv, p['norm_conv_out'][i]),
                             rms_norm(y_gmlp, p['norm_gmlp_out'][i])], axis=-1)
    x = x + y_mix @ p['w_out'][i]
    mem_n = rms_norm(mem, p['norm_mem'][i])
    x = x + cross_attention(rms_norm(x, p['norm_cross'][i]), mem_n,
                            p['xattn_w_q'][i], p['xattn_w_kv'][i], p['xattn_w_o'][i])
    x = x + 0.5 * swiglu(rms_norm(x, p['norm_ffn2'][i]), p['ffn2_w_gu'][i], p['ffn2_w_down'][i])
    return x


def encoder(x, mem, p, norm_final):
    for i in range(DEPTH):
        x = encoder_layer(x, mem, p, i)
    return rms_norm(x, norm_final)


def setup_inputs(seed: int = 0) -> dict:
    key = jax.random.key(seed)
    ks = jax.random.split(key, 32)
    f32 = jnp.float32
    nrm = lambda k, shape, scale: (jax.random.normal(k, shape, f32) * scale)
    gain = lambda k, shape: 1.0 + 0.02 * jax.random.normal(k, shape, f32)
    L = DEPTH
    return {
        'x_prompt': nrm(ks[0], (BATCH, SEQ, D_MODEL), 1.0),
        'x_sample': nrm(ks[1], (DEC_BATCH, DEC_SEQ, D_MODEL), 1.0),
        'mem_prompt': nrm(ks[2], (BATCH, N_MEM, D_MODEL), 1.0),
        'mem_sample': nrm(ks[3], (DEC_BATCH, N_MEM, D_MODEL), 1.0),
        'norm_ffn1': gain(ks[4], (L, D_MODEL)),
        'ffn1_w_gu': nrm(ks[5], (L, D_MODEL, 2 * D_FF), D_MODEL ** -0.5),
        'ffn1_w_down': nrm(ks[6], (L, D_FF, D_MODEL), D_FF ** -0.5),
        'norm_mix': gain(ks[7], (L, D_MODEL)),
        'w_in': nrm(ks[8], (L, D_MODEL, D_IN), D_MODEL ** -0.5),
        'conv_w': nrm(ks[9], (L, CONV_WIDTH, D_CONV), CONV_WIDTH ** -0.5),
        'gmlp_w_s': nrm(ks[10], (L, N_GMLP_HEADS, CHUNK, CHUNK), CHUNK ** -0.5),
        'gmlp_b_s': gain(ks[11], (L, N_GMLP_HEADS, CHUNK)),
        'gmlp_ln_g': gain(ks[12], (L, D_GMLP)),
        'gmlp_ln_b': nrm(ks[13], (L, D_GMLP), 0.02),
        'norm_conv_out': gain(ks[14], (L, D_CONV)),
        'norm_gmlp_out': gain(ks[15], (L, D_GMLP)),
        'w_out': nrm(ks[16], (L, D_MIX, D_MODEL), D_MIX ** -0.5),
        'norm_mem': gain(ks[17], (L, D_MODEL)),
        'norm_cross': gain(ks[18], (L, D_MODEL)),
        'xattn_w_q': nrm(ks[19], (L, D_MODEL, D_MODEL), D_MODEL ** -0.5),
        'xattn_w_kv': nrm(ks[20], (L, D_MODEL, 2 * D_MODEL), D_MODEL ** -0.5),
        'xattn_w_o': nrm(ks[21], (L, D_MODEL, D_MODEL), D_MODEL ** -0.5),
        'norm_ffn2': gain(ks[22], (L, D_MODEL)),
        'ffn2_w_gu': nrm(ks[23], (L, D_MODEL, 2 * D_FF), D_MODEL ** -0.5),
        'ffn2_w_down': nrm(ks[24], (L, D_FF, D_MODEL), D_FF ** -0.5),
        'norm_final': gain(ks[25], (D_MODEL,)),
    }


def reference(x_prompt, x_sample, mem_prompt, mem_sample,
              norm_ffn1, ffn1_w_gu, ffn1_w_down, norm_mix, w_in, conv_w,
              gmlp_w_s, gmlp_b_s, gmlp_ln_g, gmlp_ln_b, norm_conv_out, norm_gmlp_out,
              w_out, norm_mem, norm_cross, xattn_w_q, xattn_w_kv, xattn_w_o,
              norm_ffn2, ffn2_w_gu, ffn2_w_down, norm_final):
    p = dict(norm_ffn1=norm_ffn1, ffn1_w_gu=ffn1_w_gu, ffn1_w_down=ffn1_w_down,
             norm_mix=norm_mix, w_in=w_in, conv_w=conv_w,
             gmlp_w_s=gmlp_w_s, gmlp_b_s=gmlp_b_s, gmlp_ln_g=gmlp_ln_g, gmlp_ln_b=gmlp_ln_b,
             norm_conv_out=norm_conv_out, norm_gmlp_out=norm_gmlp_out, w_out=w_out,
             norm_mem=norm_mem, norm_cross=norm_cross, xattn_w_q=xattn_w_q,
             xattn_w_kv=xattn_w_kv, xattn_w_o=xattn_w_o,
             norm_ffn2=norm_ffn2, ffn2_w_gu=ffn2_w_gu, ffn2_w_down=ffn2_w_down)
    y_prompt = encoder(x_prompt, mem_prompt, p, norm_final)
    y_sample = encoder(x_sample, mem_sample, p, norm_final)
    return (y_prompt, y_sample)
```

```python
import functools

import jax
import jax.numpy as jnp
from jax import lax
from jax.experimental import pallas as pl
from jax.experimental.pallas import tpu as pltpu

F32 = jnp.float32
BF16 = jnp.bfloat16

EPS = 1e-6
D_MODEL = 1024
D_FF = 2816
D_CONV = 512
D_GMLP = 512
N_GMLP_HEADS = 8
CHUNK = 128
N_MEM = 256
N_XHEADS = 4
XHEAD_DIM = D_MODEL // N_XHEADS

V7X_MXU_DIM = 256
V7X_BF16_SUBLANES = 16
V7X_VMEM_BYTES = 64 << 20

FF_CHUNK = V7X_MXU_DIM
N_FF_CHUNKS = D_FF // FF_CHUNK
assert N_FF_CHUNKS * FF_CHUNK == D_FF
TOK_TILE = 512
HALO = V7X_BF16_SUBLANES
VMEM_LIMIT = V7X_VMEM_BYTES - (8 << 20)


def _rms(x, g):
    ms = jnp.mean(x * x, axis=-1, keepdims=True)
    return x * lax.rsqrt(ms + EPS) * g


def _resident(shape):
    zeros = (0,) * len(shape)
    return pl.BlockSpec(shape, lambda *_: zeros, pipeline_mode=pl.Buffered(1))


def _params():
    return pltpu.CompilerParams(vmem_limit_bytes=VMEM_LIMIT)


def _ffn_kernel(x_ref, g_ref, wgu_ref, wd_ref, gfin_ref, o_ref, *, final_norm):
    x = x_ref[...]
    n = _rms(x, g_ref[...]).astype(BF16)
    acc = None
    for c in range(N_FF_CHUNKS):
        h = jnp.dot(n, wgu_ref[c], preferred_element_type=F32)
        a = (jax.nn.silu(h[:, :FF_CHUNK]) * h[:, FF_CHUNK:]).astype(BF16)
        d = jnp.dot(a, wd_ref[c], preferred_element_type=F32)
        acc = d if acc is None else acc + d
    y = x + 0.5 * acc
    if final_norm:
        y = _rms(y, gfin_ref[...])
    o_ref[...] = y


def _ffn(x2d, g, wgu, wd, gfin, final_norm):
    n_tok = x2d.shape[0]
    row = pl.BlockSpec((TOK_TILE, D_MODEL), lambda i: (i, 0))
    return pl.pallas_call(
        functools.partial(_ffn_kernel, final_norm=final_norm),
        grid=(n_tok // TOK_TILE,),
        in_specs=[row, _resident(g.shape), _resident(wgu.shape), _resident(wd.shape),
                  _resident(gfin.shape)],
        out_specs=row,
        out_shape=jax.ShapeDtypeStruct(x2d.shape, F32),
        compiler_params=_params(),
        name="ffn_final" if final_norm else "ffn",
    )(x2d, g, wgu, wd, gfin)


def _mix_kernel(x_ref, xp_ref, xn_ref, g_ref, win_ref, cw_ref, ws_ref, bs_ref, lng_ref,
                lnb_ref, gco_ref, ggo_ref, wout_ref, o_ref, next_scr, z_scr, ymix_scr):
    t = TOK_TILE
    i = pl.program_id(1)
    x = x_ref[...]
    g = g_ref[...]
    n = _rms(x, g).astype(BF16)
    next_scr[0:HALO, :] = _rms(xp_ref[...], g).astype(BF16)
    next_scr[HALO:HALO + t, :] = n
    next_scr[HALO + t:, :] = _rms(xn_ref[...], g).astype(BF16)

    pcx = jnp.dot(next_scr[...], win_ref[:, D_CONV:3 * D_CONV], preferred_element_type=F32)
    z_scr[...] = pcx[:, :D_CONV] * pcx[:, D_CONV:]

    @pl.when(i == 0)
    def _():
        z_scr[0:HALO, :] = jnp.zeros((HALO, D_CONV), F32)

    @pl.when(i == pl.num_programs(1) - 1)
    def _():
        z_scr[HALO + t:, :] = jnp.zeros((HALO, D_CONV), F32)

    cw = cw_ref[...]
    conv = (cw[0:1] * z_scr[HALO - 1:HALO - 1 + t, :] + cw[1:2] * z_scr[HALO:HALO + t, :]
            + cw[2:3] * z_scr[HALO + 1:HALO + 1 + t, :])
    pb = jnp.dot(n, win_ref[:, 0:D_CONV], preferred_element_type=F32)
    ymix_scr[:, 0:D_CONV] = _rms(pb * conv, gco_ref[...]).astype(BF16)

    puv = jnp.dot(n, win_ref[:, 3 * D_CONV:], preferred_element_type=F32)
    gu = jax.nn.gelu(puv[:, :D_GMLP])
    gv = jax.nn.gelu(puv[:, D_GMLP:])
    mu = jnp.mean(gv, axis=-1, keepdims=True)
    vc = gv - mu
    vln = (vc * lax.rsqrt(jnp.mean(vc * vc, axis=-1, keepdims=True) + EPS) * lng_ref[...]
           + lnb_ref[...]).astype(BF16)
    first_head = lax.broadcasted_iota(jnp.int32, (CHUNK, 128), 1) < (D_GMLP // N_GMLP_HEADS)
    chunks = []
    for c in range(t // CHUNK):
        vck = vln[c * CHUNK:(c + 1) * CHUNK]
        pairs = []
        for j in range(N_GMLP_HEADS // 2):
            r = jnp.dot(ws_ref[j], vck[:, j * 128:(j + 1) * 128], preferred_element_type=F32)
            pairs.append(jnp.where(first_head, r[:CHUNK], r[CHUNK:]))
        chunks.append(jnp.concatenate(pairs, axis=1) + bs_ref[...])
    mixed = jnp.concatenate(chunks, axis=0)
    ymix_scr[:, D_CONV:] = _rms(gu * mixed, ggo_ref[...]).astype(BF16)

    o_ref[...] = x + jnp.dot(ymix_scr[...], wout_ref[...], preferred_element_type=F32)


def _mix(x3d, g, win, cw, ws, bs, lng, lnb, gco, ggo, wout):
    b, s, _ = x3d.shape
    t = TOK_TILE
    per_tile = t // HALO
    n_halo_blocks = s // HALO
    row = pl.BlockSpec((None, t, D_MODEL), lambda bi, i: (bi, i, 0))
    prev = pl.BlockSpec((None, HALO, D_MODEL),
                        lambda bi, i: (bi, jnp.maximum(i * per_tile - 1, 0), 0))
    nxt = pl.BlockSpec((None, HALO, D_MODEL),
                       lambda bi, i: (bi, jnp.minimum((i + 1) * per_tile, n_halo_blocks - 1), 0))
    consts = (g, win, cw, ws, bs, lng, lnb, gco, ggo, wout)
    return pl.pallas_call(
        _mix_kernel,
        grid=(b, s // t),
        in_specs=[row, prev, nxt] + [_resident(a.shape) for a in consts],
        out_specs=row,
        out_shape=jax.ShapeDtypeStruct(x3d.shape, F32),
        scratch_shapes=[pltpu.VMEM((t + 2 * HALO, D_MODEL), BF16),
                        pltpu.VMEM((t + 2 * HALO, D_CONV), F32),
                        pltpu.VMEM((t, D_MODEL), BF16)],
        compiler_params=_params(),
        name="mix",
    )(x3d, x3d, x3d, *consts)


def _kv_kernel(mem_ref, g_ref, wkv_ref, kt_ref, v_ref):
    n = _rms(mem_ref[...], g_ref[...]).astype(BF16)
    kv = jnp.dot(n, wkv_ref[...], preferred_element_type=F32)
    for h in range(N_XHEADS):
        kt_ref[h] = kv[:, h * XHEAD_DIM:(h + 1) * XHEAD_DIM].T.astype(BF16)
    v_ref[...] = kv[:, D_MODEL:].astype(BF16)


def _kv(mem, g, wkv):
    b = mem.shape[0]
    return pl.pallas_call(
        _kv_kernel,
        grid=(b,),
        in_specs=[pl.BlockSpec((None, N_MEM, D_MODEL), lambda bi: (bi, 0, 0)),
                  _resident(g.shape), _resident(wkv.shape)],
        out_specs=[pl.BlockSpec((None, N_XHEADS, XHEAD_DIM, N_MEM), lambda bi: (bi, 0, 0, 0)),
                   pl.BlockSpec((None, N_MEM, D_MODEL), lambda bi: (bi, 0, 0))],
        out_shape=[jax.ShapeDtypeStruct((b, N_XHEADS, XHEAD_DIM, N_MEM), BF16),
                   jax.ShapeDtypeStruct((b, N_MEM, D_MODEL), BF16)],
        compiler_params=_params(),
        name="kv_proj",
    )(mem, g, wkv)


def _xattn_kernel(x_ref, g_ref, wq_ref, kt_ref, v_ref, wo_ref, o_ref, o_scr):
    x = x_ref[...]
    n = _rms(x, g_ref[...]).astype(BF16)
    q = (jnp.dot(n, wq_ref[...], preferred_element_type=F32) * (XHEAD_DIM ** -0.5)).astype(BF16)
    for h in range(N_XHEADS):
        cols = slice(h * XHEAD_DIM, (h + 1) * XHEAD_DIM)
        s = jnp.dot(q[:, cols], kt_ref[h], preferred_element_type=F32)
        p = jnp.exp(s - jnp.max(s, axis=-1, keepdims=True))
        attn = (p * (1.0 / jnp.sum(p, axis=-1, keepdims=True))).astype(BF16)
        o_scr[:, cols] = jnp.dot(attn, v_ref[:, cols], preferred_element_type=F32).astype(BF16)
    o_ref[...] = x + jnp.dot(o_scr[...], wo_ref[...], preferred_element_type=F32)


def _xattn(x3d, g, wq, kt, v, wo):
    b, s, _ = x3d.shape
    t = TOK_TILE
    row = pl.BlockSpec((None, t, D_MODEL), lambda bi, i: (bi, i, 0))
    return pl.pallas_call(
        _xattn_kernel,
        grid=(b, s // t),
        in_specs=[row, _resident(g.shape), _resident(wq.shape),
                  pl.BlockSpec((None, N_XHEADS, XHEAD_DIM, N_MEM), lambda bi, i: (bi, 0, 0, 0)),
                  pl.BlockSpec((None, N_MEM, D_MODEL), lambda bi, i: (bi, 0, 0)),
                  _resident(wo.shape)],
        out_specs=row,
        out_shape=jax.ShapeDtypeStruct(x3d.shape, F32),
        scratch_shapes=[pltpu.VMEM((t, D_MODEL), BF16)],
        compiler_params=_params(),
        name="xattn",
    )(x3d, g, wq, kt, v, wo)


def _chunk_ffn_weights(w_gu, w_down):
    gate = w_gu[:, :D_FF].reshape(D_MODEL, N_FF_CHUNKS, FF_CHUNK)
    up = w_gu[:, D_FF:].reshape(D_MODEL, N_FF_CHUNKS, FF_CHUNK)
    wgu = jnp.concatenate([gate, up], axis=-1).transpose(1, 0, 2).astype(BF16)
    wd = w_down.reshape(N_FF_CHUNKS, FF_CHUNK, D_MODEL).astype(BF16)
    return wgu, wd


def kernel(x_prompt, x_sample, mem_prompt, mem_sample, norm_ffn1, ffn1_w_gu, ffn1_w_down, norm_mix, w_in, conv_w, gmlp_w_s, gmlp_b_s, gmlp_ln_g, gmlp_ln_b, norm_conv_out, norm_gmlp_out, w_out, norm_mem, norm_cross, xattn_w_q, xattn_w_kv, xattn_w_o, norm_ffn2, ffn2_w_gu, ffn2_w_down, norm_final):
    assert ffn1_w_gu.shape[0] == 1, "single layer"
    row = lambda a: a.reshape(1, -1).astype(F32)
    wgu1, wd1 = _chunk_ffn_weights(ffn1_w_gu[0], ffn1_w_down[0])
    wgu2, wd2 = _chunk_ffn_weights(ffn2_w_gu[0], ffn2_w_down[0])
    win = w_in[0].astype(BF16)
    wout = w_out[0].astype(BF16)
    ws = gmlp_w_s[0].reshape(N_GMLP_HEADS // 2, 2 * CHUNK, CHUNK).astype(BF16)
    bs = jnp.repeat(gmlp_b_s[0].T, D_GMLP // N_GMLP_HEADS, axis=1).astype(F32)
    wq = xattn_w_q[0].astype(BF16)
    wkv = xattn_w_kv[0].astype(BF16)
    wo = xattn_w_o[0].astype(BF16)
    g_ffn1, g_mix, g_mem, g_cross, g_ffn2, g_fin = (
        row(norm_ffn1[0]), row(norm_mix[0]), row(norm_mem[0]), row(norm_cross[0]),
        row(norm_ffn2[0]), row(norm_final))
    cw = conv_w[0].astype(F32)
    lng, lnb = row(gmlp_ln_g[0]), row(gmlp_ln_b[0])
    gco, ggo = row(norm_conv_out[0]), row(norm_gmlp_out[0])

    def encode(x, mem):
        b, s, d = x.shape
        x1 = _ffn(x.reshape(b * s, d), g_ffn1, wgu1, wd1, g_fin, False).reshape(b, s, d)
        x2 = _mix(x1, g_mix, win, cw, ws, bs, lng, lnb, gco, ggo, wout)
        kt, v = _kv(mem, g_mem, wkv)
        x3 = _xattn(x2, g_cross, wq, kt, v, wo)
        y = _ffn(x3.reshape(b * s, d), g_ffn2, wgu2, wd2, g_fin, True)
        return y.reshape(b, s, d)

    return encode(x_prompt, mem_prompt), encode(x_sample, mem_sample)
```

```python
import functools

import jax
import jax.numpy as jnp
from jax import lax
from jax.experimental import pallas as pl
from jax.experimental.pallas import tpu as pltpu

F32 = jnp.float32
BF16 = jnp.bfloat16

EPS = 1e-6
D_MODEL = 1024
D_FF = 2816
D_CONV = 512
D_GMLP = 512
N_GMLP_HEADS = 8
CHUNK = 128
N_MEM = 256
N_XHEADS = 4
XHEAD_DIM = D_MODEL // N_XHEADS

V7X_MXU_DIM = 256
V7X_BF16_SUBLANES = 16
V7X_VMEM_BYTES = 64 << 20

FF_CHUNK = V7X_MXU_DIM
N_FF_CHUNKS = D_FF // FF_CHUNK
assert N_FF_CHUNKS * FF_CHUNK == D_FF
FFN_TILE = 1024
MIX_TILE = 1024
XATTN_TILE = 1024
HALO = V7X_BF16_SUBLANES
VMEM_LIMIT = V7X_VMEM_BYTES - (8 << 20)


def _rms(x, g):
    ms = jnp.mean(x * x, axis=-1, keepdims=True)
    return x * lax.rsqrt(ms + EPS) * g


def _resident(shape):
    zeros = (0,) * len(shape)
    return pl.BlockSpec(shape, lambda *_: zeros, pipeline_mode=pl.Buffered(1))


def _params():
    return pltpu.CompilerParams(vmem_limit_bytes=VMEM_LIMIT)


class _Stream:
    def __init__(self, shape_a, shape_b, t):
        (ba, sa, _), (bb, sb, _) = shape_a, shape_b
        assert sa % t == 0 and sb % t == 0 and t % CHUNK == 0
        self.t = t
        self.tiles_a, self.tiles_b = ba * sa // t, bb * sb // t
        self.per_a, self.per_b = sa // t, sb // t
        self.seqs_a = ba
        self.n_tiles = self.tiles_a + self.tiles_b
        self.n_tok = self.n_tiles * t

    def seq_pos(self, j):
        in_a = j < self.tiles_a
        k = jnp.where(in_a, j, j - self.tiles_a)
        per = jnp.where(in_a, self.per_a, self.per_b)
        seq = jnp.where(in_a, k // self.per_a, self.seqs_a + k // self.per_b)
        return seq, k % per, per


def _ffn_kernel(*refs, tiles_a, final_norm):
    *x_refs, g_ref, wgu_ref, wd_ref, gfin_ref, o_ref = refs
    if len(x_refs) == 2:
        x = jnp.where(pl.program_id(0) < tiles_a, x_refs[0][...], x_refs[1][...])
    else:
        x = x_refs[0][...]
    n = _rms(x, g_ref[...]).astype(BF16)
    acc = None
    for c in range(N_FF_CHUNKS):
        gate = jnp.dot(n, wgu_ref[:, c * FF_CHUNK:(c + 1) * FF_CHUNK],
                       preferred_element_type=F32)
        up = jnp.dot(n, wgu_ref[:, D_FF + c * FF_CHUNK:D_FF + (c + 1) * FF_CHUNK],
                     preferred_element_type=F32)
        a = (jax.nn.silu(gate) * up).astype(BF16)
        d = jnp.dot(a, wd_ref[c * FF_CHUNK:(c + 1) * FF_CHUNK, :], preferred_element_type=F32)
        acc = d if acc is None else acc + d
    y = x + 0.5 * acc
    if final_norm:
        y = _rms(y, gfin_ref[...])
    o_ref[...] = y


def _ffn_call(x_arrays, x_specs, n_tiles, tiles_a, consts, final_norm):
    t = FFN_TILE
    return pl.pallas_call(
        functools.partial(_ffn_kernel, tiles_a=tiles_a, final_norm=final_norm),
        grid=(n_tiles,),
        in_specs=list(x_specs) + [_resident(a.shape) for a in consts],
        out_specs=pl.BlockSpec((t, D_MODEL), lambda j: (j, 0)),
        out_shape=jax.ShapeDtypeStruct((n_tiles * t, D_MODEL), F32),
        compiler_params=_params(),
        name="ffn_final" if final_norm else "ffn",
    )(*x_arrays, *consts)


def _ffn_first(xa, xb, consts):
    t = FFN_TILE
    st = _Stream(xa.shape, xb.shape, t)
    ta = st.tiles_a
    specs = [pl.BlockSpec((t, D_MODEL), lambda j: (jnp.minimum(j, ta - 1), 0)),
             pl.BlockSpec((t, D_MODEL), lambda j: (jnp.maximum(j - ta, 0), 0))]
    arrays = [xa.reshape(-1, D_MODEL), xb.reshape(-1, D_MODEL)]
    return _ffn_call(arrays, specs, st.n_tiles, ta, consts, False)


def _ffn_last(x, first_tile, n_tiles, consts):
    spec = pl.BlockSpec((FFN_TILE, D_MODEL), lambda j: (j + first_tile, 0))
    return _ffn_call([x], [spec], n_tiles, 0, consts, True)


def _mix_kernel(x_ref, xp_ref, xn_ref, g_ref, win_ref, cw_ref, ws_ref, bs_ref, lng_ref,
                lnb_ref, gco_ref, ggo_ref, wout_ref, o_ref, next_scr, z_scr, ymix_scr, *, st):
    t = st.t
    _, pos, per = st.seq_pos(pl.program_id(0))
    x = x_ref[...]
    g = g_ref[...]
    n = _rms(x, g).astype(BF16)
    next_scr[0:HALO, :] = _rms(xp_ref[...], g).astype(BF16)
    next_scr[HALO:HALO + t, :] = n
    next_scr[HALO + t:, :] = _rms(xn_ref[...], g).astype(BF16)

    pcx = jnp.dot(next_scr[...], win_ref[:, D_CONV:3 * D_CONV], preferred_element_type=F32)
    z_scr[...] = pcx[:, :D_CONV] * pcx[:, D_CONV:]

    @pl.when(pos == 0)
    def _():
        z_scr[0:HALO, :] = jnp.zeros((HALO, D_CONV), F32)

    @pl.when(pos == per - 1)
    def _():
        z_scr[HALO + t:, :] = jnp.zeros((HALO, D_CONV), F32)

    cw = cw_ref[...]
    conv = (cw[0:1] * z_scr[HALO - 1:HALO - 1 + t, :] + cw[1:2] * z_scr[HALO:HALO + t, :]
            + cw[2:3] * z_scr[HALO + 1:HALO + 1 + t, :])
    pb = jnp.dot(n, win_ref[:, 0:D_CONV], preferred_element_type=F32)
    ymix_scr[:, 0:D_CONV] = _rms(pb * conv, gco_ref[...]).astype(BF16)

    puv = jnp.dot(n, win_ref[:, 3 * D_CONV:], preferred_element_type=F32)
    gu = jax.nn.gelu(puv[:, :D_GMLP])
    gv = jax.nn.gelu(puv[:, D_GMLP:])
    mu = jnp.mean(gv, axis=-1, keepdims=True)
    vc = gv - mu
    vln = (vc * lax.rsqrt(jnp.mean(vc * vc, axis=-1, keepdims=True) + EPS) * lng_ref[...]
           + lnb_ref[...]).astype(BF16)
    first_head = lax.broadcasted_iota(jnp.int32, (CHUNK, 128), 1) < (D_GMLP // N_GMLP_HEADS)
    chunks = []
    for c in range(t // CHUNK):
        vck = vln[c * CHUNK:(c + 1) * CHUNK]
        pairs = []
        for j in range(N_GMLP_HEADS // 2):
            r = jnp.dot(ws_ref[j], vck[:, j * 128:(j + 1) * 128], preferred_element_type=F32)
            pairs.append(jnp.where(first_head, r[:CHUNK], r[CHUNK:]))
        chunks.append(jnp.concatenate(pairs, axis=1) + bs_ref[...])
    mixed = jnp.concatenate(chunks, axis=0)
    ymix_scr[:, D_CONV:] = _rms(gu * mixed, ggo_ref[...]).astype(BF16)

    o_ref[...] = x + jnp.dot(ymix_scr[...], wout_ref[...], preferred_element_type=F32)


def _mix(x, st, consts):
    t = st.t
    per_tile = t // HALO
    last_halo = st.n_tok // HALO - 1
    row = pl.BlockSpec((t, D_MODEL), lambda j: (j, 0))
    prev = pl.BlockSpec((HALO, D_MODEL), lambda j: (jnp.maximum(j * per_tile - 1, 0), 0))
    nxt = pl.BlockSpec((HALO, D_MODEL), lambda j: (jnp.minimum((j + 1) * per_tile, last_halo), 0))
    return pl.pallas_call(
        functools.partial(_mix_kernel, st=st),
        grid=(st.n_tiles,),
        in_specs=[row, prev, nxt] + [_resident(a.shape) for a in consts],
        out_specs=row,
        out_shape=jax.ShapeDtypeStruct(x.shape, F32),
        scratch_shapes=[pltpu.VMEM((t + 2 * HALO, D_MODEL), BF16),
                        pltpu.VMEM((t + 2 * HALO, D_CONV), F32),
                        pltpu.VMEM((t, D_MODEL), BF16)],
        compiler_params=_params(),
        name="mix",
    )(x, x, x, *consts)


def _kv_kernel(mem_ref, g_ref, wkv_ref, kt_ref, v_ref):
    n = _rms(mem_ref[...], g_ref[...]).astype(BF16)
    kv = jnp.dot(n, wkv_ref[...], preferred_element_type=F32)
    for h in range(N_XHEADS):
        kt_ref[h] = kv[:, h * XHEAD_DIM:(h + 1) * XHEAD_DIM].T.astype(BF16)
    v_ref[...] = kv[:, D_MODEL:].astype(BF16)


def _kv(mem, g, wkv):
    b = mem.shape[0]
    return pl.pallas_call(
        _kv_kernel,
        grid=(b,),
        in_specs=[pl.BlockSpec((None, N_MEM, D_MODEL), lambda bi: (bi, 0, 0)),
                  _resident(g.shape), _resident(wkv.shape)],
        out_specs=[pl.BlockSpec((None, N_XHEADS, XHEAD_DIM, N_MEM), lambda bi: (bi, 0, 0, 0)),
                   pl.BlockSpec((None, N_MEM, D_MODEL), lambda bi: (bi, 0, 0))],
        out_shape=[jax.ShapeDtypeStruct((b, N_XHEADS, XHEAD_DIM, N_MEM), BF16),
                   jax.ShapeDtypeStruct((b, N_MEM, D_MODEL), BF16)],
        compiler_params=_params(),
        name="kv_proj",
    )(mem, g, wkv)


def _xattn_kernel(x_ref, g_ref, wq_ref, kt_ref, v_ref, wo_ref, o_ref, o_scr):
    x = x_ref[...]
    n = _rms(x, g_ref[...]).astype(BF16)
    q = (jnp.dot(n, wq_ref[...], preferred_element_type=F32) * (XHEAD_DIM ** -0.5)).astype(BF16)
    for h in range(N_XHEADS):
        cols = slice(h * XHEAD_DIM, (h + 1) * XHEAD_DIM)
        s = jnp.dot(q[:, cols], kt_ref[h], preferred_element_type=F32)
        p = jnp.exp(s - jnp.max(s, axis=-1, keepdims=True))
        attn = (p * (1.0 / jnp.sum(p, axis=-1, keepdims=True))).astype(BF16)
        o_scr[:, cols] = jnp.dot(attn, v_ref[:, cols], preferred_element_type=F32).astype(BF16)
    o_ref[...] = x + jnp.dot(o_scr[...], wo_ref[...], preferred_element_type=F32)


def _xattn(x, st, g, wq, kt, v, wo):
    t = st.t
    row = pl.BlockSpec((t, D_MODEL), lambda j: (j, 0))
    seq = lambda j: st.seq_pos(j)[0]
    return pl.pallas_call(
        _xattn_kernel,
        grid=(st.n_tiles,),
        in_specs=[row, _resident(g.shape), _resident(wq.shape),
                  pl.BlockSpec((None, N_XHEADS, XHEAD_DIM, N_MEM), lambda j: (seq(j), 0, 0, 0)),
                  pl.BlockSpec((None, N_MEM, D_MODEL), lambda j: (seq(j), 0, 0)),
                  _resident(wo.shape)],
        out_specs=row,
        out_shape=jax.ShapeDtypeStruct(x.shape, F32),
        scratch_shapes=[pltpu.VMEM((t, D_MODEL), BF16)],
        compiler_params=_params(),
        name="xattn",
    )(x, g, wq, kt, v, wo)


def kernel(x_prompt, x_sample, mem_prompt, mem_sample, norm_ffn1, ffn1_w_gu, ffn1_w_down, norm_mix, w_in, conv_w, gmlp_w_s, gmlp_b_s, gmlp_ln_g, gmlp_ln_b, norm_conv_out, norm_gmlp_out, w_out, norm_mem, norm_cross, xattn_w_q, xattn_w_kv, xattn_w_o, norm_ffn2, ffn2_w_gu, ffn2_w_down, norm_final):
    assert ffn1_w_gu.shape[0] == 1, "single layer"
    row = lambda a: a.reshape(1, -1).astype(F32)
    bf = lambda a: a[0].astype(BF16)
    ws = gmlp_w_s[0].reshape(N_GMLP_HEADS // 2, 2 * CHUNK, CHUNK).astype(BF16)
    bs = jnp.repeat(gmlp_b_s[0].T, D_GMLP // N_GMLP_HEADS, axis=1).astype(F32)
    g_fin = row(norm_final)
    ffn1 = (row(norm_ffn1[0]), bf(ffn1_w_gu), bf(ffn1_w_down), g_fin)
    ffn2 = (row(norm_ffn2[0]), bf(ffn2_w_gu), bf(ffn2_w_down), g_fin)
    mix = (row(norm_mix[0]), bf(w_in), conv_w[0].astype(F32), ws, bs, row(gmlp_ln_g[0]),
           row(gmlp_ln_b[0]), row(norm_conv_out[0]), row(norm_gmlp_out[0]), bf(w_out))

    x1 = _ffn_first(x_prompt, x_sample, ffn1)
    x2 = _mix(x1, _Stream(x_prompt.shape, x_sample.shape, MIX_TILE), mix)
    mem = jnp.concatenate([mem_prompt, mem_sample], axis=0)
    kt, v = _kv(mem, row(norm_mem[0]), bf(xattn_w_kv))
    x3 = _xattn(x2, _Stream(x_prompt.shape, x_sample.shape, XATTN_TILE), row(norm_cross[0]),
                bf(xattn_w_q), kt, v, bf(xattn_w_o))
    st = _Stream(x_prompt.shape, x_sample.shape, FFN_TILE)
    y_prompt = _ffn_last(x3, 0, st.tiles_a, ffn2).reshape(x_prompt.shape)
    y_sample = _ffn_last(x3, st.tiles_a, st.tiles_b, ffn2).reshape(x_sample.shape)
    return y_prompt, y_sample
```

```python
import functools

import jax
import jax.numpy as jnp
from jax import lax
from jax.experimental import pallas as pl
from jax.experimental.pallas import tpu as pltpu

F32 = jnp.float32
BF16 = jnp.bfloat16

EPS = 1e-6
D_MODEL = 1024
D_FF = 2816
D_CONV = 512
D_GMLP = 512
N_GMLP_HEADS = 8
CHUNK = 128
N_MEM = 256
N_XHEADS = 4
XHEAD_DIM = D_MODEL // N_XHEADS

V7X_MXU_DIM = 256
V7X_BF16_SUBLANES = 16
V7X_VMEM_BYTES = 64 << 20

FF_CHUNK = V7X_MXU_DIM
N_FF_CHUNKS = D_FF // FF_CHUNK
assert N_FF_CHUNKS * FF_CHUNK == D_FF
FFN_TILE = 1024
MIX_TILE = 1024
XATTN_TILE = 1024
HALO = V7X_BF16_SUBLANES
VMEM_LIMIT = V7X_VMEM_BYTES - (8 << 20)


def _rms(x, g):
    ms = jnp.mean(x * x, axis=-1, keepdims=True)
    return x * lax.rsqrt(ms + EPS) * g


def _resident(shape):
    zeros = (0,) * len(shape)
    return pl.BlockSpec(shape, lambda *_: zeros, pipeline_mode=pl.Buffered(1))


def _params():
    return pltpu.CompilerParams(vmem_limit_bytes=VMEM_LIMIT)


def _slot_scratch(t):
    return [pltpu.VMEM((t, D_MODEL), F32), pltpu.VMEM((t, D_MODEL), F32),
            pltpu.VMEM((t, D_MODEL), BF16), pltpu.VMEM((t, D_MODEL), BF16)]


class _Stream:
    def __init__(self, shape_a, shape_b, t):
        (ba, sa, _), (bb, sb, _) = shape_a, shape_b
        assert sa % t == 0 and sb % t == 0 and t % CHUNK == 0
        self.t = t
        self.tiles_a, self.tiles_b = ba * sa // t, bb * sb // t
        self.per_a, self.per_b = sa // t, sb // t
        self.seqs_a = ba
        self.n_tiles = self.tiles_a + self.tiles_b
        self.n_tok = self.n_tiles * t

    def seq_pos(self, j):
        in_a = j < self.tiles_a
        k = jnp.where(in_a, j, j - self.tiles_a)
        per = jnp.where(in_a, self.per_a, self.per_b)
        seq = jnp.where(in_a, k // self.per_a, self.seqs_a + k // self.per_b)
        return seq, k % per, per


def _ffn_kernel(*refs, tiles_a, final_norm):
    *x_refs, g_ref, wgu_ref, wd_ref, gfin_ref, o_ref = refs
    if len(x_refs) == 2:
        x = jnp.where(pl.program_id(0) < tiles_a, x_refs[0][...], x_refs[1][...])
    else:
        x = x_refs[0][...]
    n = _rms(x, g_ref[...]).astype(BF16)
    acc = None
    for c in range(N_FF_CHUNKS):
        gate = jnp.dot(n, wgu_ref[:, c * FF_CHUNK:(c + 1) * FF_CHUNK],
                       preferred_element_type=F32)
        up = jnp.dot(n, wgu_ref[:, D_FF + c * FF_CHUNK:D_FF + (c + 1) * FF_CHUNK],
                     preferred_element_type=F32)
        a = (jax.nn.silu(gate) * up).astype(BF16)
        d = jnp.dot(a, wd_ref[c * FF_CHUNK:(c + 1) * FF_CHUNK, :], preferred_element_type=F32)
        acc = d if acc is None else acc + d
    y = x + 0.5 * acc
    if final_norm:
        y = _rms(y, gfin_ref[...])
    o_ref[...] = y


def _ffn_call(x_arrays, x_specs, n_tiles, tiles_a, consts, final_norm):
    t = FFN_TILE
    return pl.pallas_call(
        functools.partial(_ffn_kernel, tiles_a=tiles_a, final_norm=final_norm),
        grid=(n_tiles,),
        in_specs=list(x_specs) + [_resident(a.shape) for a in consts],
        out_specs=pl.BlockSpec((t, D_MODEL), lambda j: (j, 0)),
        out_shape=jax.ShapeDtypeStruct((n_tiles * t, D_MODEL), F32),
        compiler_params=_params(),
        name="ffn_final" if final_norm else "ffn",
    )(*x_arrays, *consts)


def _ffn_first(xa, xb, consts):
    t = FFN_TILE
    st = _Stream(xa.shape, xb.shape, t)
    ta = st.tiles_a
    specs = [pl.BlockSpec((t, D_MODEL), lambda j: (jnp.minimum(j, ta - 1), 0)),
             pl.BlockSpec((t, D_MODEL), lambda j: (jnp.maximum(j - ta, 0), 0))]
    arrays = [xa.reshape(-1, D_MODEL), xb.reshape(-1, D_MODEL)]
    return _ffn_call(arrays, specs, st.n_tiles, ta, consts, False)


def _ffn_last(x, first_tile, n_tiles, consts):
    spec = pl.BlockSpec((FFN_TILE, D_MODEL), lambda j: (j + first_tile, 0))
    return _ffn_call([x], [spec], n_tiles, 0, consts, True)


class _Pieces:
    def __init__(self, thunks=()):
        self._thunks = list(thunks)

    def run(self, k=None):
        k = len(self._thunks) if k is None else min(k, len(self._thunks))
        for thunk in self._thunks[:k]:
            thunk()
        del self._thunks[:k]


def _two_phase(j, n_tiles, first, second_pieces, slots):
    @pl.when(j == 0)
    def _():
        first(slots[0], _Pieces())

    for parity in (0, 1):
        @pl.when(jnp.logical_and(jnp.logical_and(j > 0, j < n_tiles), j % 2 == parity))
        def _():
            pieces = second_pieces(slots[1 - parity])
            first(slots[parity], pieces)
            pieces.run()

    @pl.when(j == n_tiles)
    def _():
        second_pieces(slots[(n_tiles - 1) % 2]).run()


def _project_out_pieces(w_ref, o_ref, slot):
    x_scr, y_scr = slot
    half = x_scr.shape[0] // 2

    def piece(k, r):
        rows = slice(r * half, (r + 1) * half)
        cols = slice(k * V7X_MXU_DIM, (k + 1) * V7X_MXU_DIM)
        o_ref[rows, cols] = x_scr[rows, cols] + jnp.dot(y_scr[rows, :], w_ref[:, cols],
                                                        preferred_element_type=F32)

    return _Pieces(functools.partial(piece, k, r)
                   for k in range(D_MODEL // V7X_MXU_DIM) for r in range(2))


def _mix_kernel(x_ref, xp_ref, xn_ref, g_ref, win_ref, cw_ref, ws_ref, bs_ref, lng_ref,
                lnb_ref, gco_ref, ggo_ref, wout_ref, o_ref, next_scr, z_scr,
                x0, x1, y0, y1, *, st):
    j = pl.program_id(0)
    _two_phase(j, st.n_tiles,
               functools.partial(_mix_first, x_ref, xp_ref, xn_ref, g_ref, win_ref, cw_ref, ws_ref,
                                 bs_ref, lng_ref, lnb_ref, gco_ref, ggo_ref, next_scr, z_scr,
                                 st=st, j=j),
               functools.partial(_project_out_pieces, wout_ref, o_ref),
               ((x0, y0), (x1, y1)))


def _mix_first(x_ref, xp_ref, xn_ref, g_ref, win_ref, cw_ref, ws_ref, bs_ref, lng_ref, lnb_ref,
               gco_ref, ggo_ref, next_scr, z_scr, slot, pieces, *, st, j):
    t = st.t
    x_scr, ymix_scr = slot
    _, pos, per = st.seq_pos(j)
    pieces.run(3)
    x = x_ref[...]
    x_scr[...] = x
    g = g_ref[...]
    n = _rms(x, g).astype(BF16)
    next_scr[0:HALO, :] = _rms(xp_ref[...], g).astype(BF16)
    next_scr[HALO:HALO + t, :] = n
    next_scr[HALO + t:, :] = _rms(xn_ref[...], g).astype(BF16)

    puv = jnp.dot(n, win_ref[:, 3 * D_CONV:], preferred_element_type=F32)
    gu = jax.nn.gelu(puv[:, :D_GMLP])
    gv = jax.nn.gelu(puv[:, D_GMLP:])
    mu = jnp.mean(gv, axis=-1, keepdims=True)
    vc = gv - mu
    vln = (vc * lax.rsqrt(jnp.mean(vc * vc, axis=-1, keepdims=True) + EPS) * lng_ref[...]
           + lnb_ref[...]).astype(BF16)

    pcx = jnp.dot(next_scr[...], win_ref[:, D_CONV:3 * D_CONV], preferred_element_type=F32)
    z = pcx[:, :D_CONV] * pcx[:, D_CONV:]
    z_scr[0:HALO, :] = jnp.where(pos == 0, 0.0, z[0:HALO])
    z_scr[HALO:HALO + t, :] = z[HALO:HALO + t]
    z_scr[HALO + t:, :] = jnp.where(pos == per - 1, 0.0, z[HALO + t:])

    cw = cw_ref[...]
    conv = (cw[0:1] * z_scr[HALO - 1:HALO - 1 + t, :] + cw[1:2] * z_scr[HALO:HALO + t, :]
            + cw[2:3] * z_scr[HALO + 1:HALO + 1 + t, :])
    pb = jnp.dot(n, win_ref[:, 0:D_CONV], preferred_element_type=F32)
    ymix_scr[:, 0:D_CONV] = _rms(pb * conv, gco_ref[...]).astype(BF16)

    first_head = lax.broadcasted_iota(jnp.int32, (CHUNK, 128), 1) < (D_GMLP // N_GMLP_HEADS)
    chunks = []
    n_chunks = t // CHUNK
    for c in range(n_chunks):
        if c % 2 == 0 and c < 6:
            pieces.run(1)
        vck = vln[c * CHUNK:(c + 1) * CHUNK]
        pairs = []
        for hp in range(N_GMLP_HEADS // 2):
            r = jnp.dot(ws_ref[hp], vck[:, hp * 128:(hp + 1) * 128], preferred_element_type=F32)
            pairs.append(jnp.where(first_head, r[:CHUNK], r[CHUNK:]))
        chunks.append(jnp.concatenate(pairs, axis=1) + bs_ref[...])
    mixed = jnp.concatenate(chunks, axis=0)
    ymix_scr[:, D_CONV:] = _rms(gu * mixed, ggo_ref[...]).astype(BF16)


def _mix(x, st, consts):
    t = st.t
    per_tile = t // HALO
    last_halo = st.n_tok // HALO - 1
    tile = lambda j: jnp.minimum(j, st.n_tiles - 1)
    row = pl.BlockSpec((t, D_MODEL), lambda j: (tile(j), 0))
    prev = pl.BlockSpec((HALO, D_MODEL), lambda j: (jnp.maximum(tile(j) * per_tile - 1, 0), 0))
    nxt = pl.BlockSpec((HALO, D_MODEL),
                       lambda j: (jnp.minimum((tile(j) + 1) * per_tile, last_halo), 0))
    return pl.pallas_call(
        functools.partial(_mix_kernel, st=st),
        grid=(st.n_tiles + 1,),
        in_specs=[row, prev, nxt] + [_resident(a.shape) for a in consts],
        out_specs=pl.BlockSpec((t, D_MODEL), lambda j: (jnp.maximum(j - 1, 0), 0)),
        out_shape=jax.ShapeDtypeStruct(x.shape, F32),
        scratch_shapes=[pltpu.VMEM((t + 2 * HALO, D_MODEL), BF16),
                        pltpu.VMEM((t + 2 * HALO, D_CONV), F32)] + _slot_scratch(t),
        compiler_params=_params(),
        name="mix",
    )(x, x, x, *consts)


def _kv_kernel(mem_ref, g_ref, wkv_ref, kt_ref, v_ref):
    n = _rms(mem_ref[...], g_ref[...]).astype(BF16)
    kv = jnp.dot(n, wkv_ref[...], preferred_element_type=F32)
    for h in range(N_XHEADS):
        kt_ref[h] = kv[:, h * XHEAD_DIM:(h + 1) * XHEAD_DIM].T.astype(BF16)
    v_ref[...] = kv[:, D_MODEL:].astype(BF16)


def _kv(mem, g, wkv):
    b = mem.shape[0]
    return pl.pallas_call(
        _kv_kernel,
        grid=(b,),
        in_specs=[pl.BlockSpec((None, N_MEM, D_MODEL), lambda bi: (bi, 0, 0)),
                  _resident(g.shape), _resident(wkv.shape)],
        out_specs=[pl.BlockSpec((None, N_XHEADS, XHEAD_DIM, N_MEM), lambda bi: (bi, 0, 0, 0)),
                   pl.BlockSpec((None, N_MEM, D_MODEL), lambda bi: (bi, 0, 0))],
        out_shape=[jax.ShapeDtypeStruct((b, N_XHEADS, XHEAD_DIM, N_MEM), BF16),
                   jax.ShapeDtypeStruct((b, N_MEM, D_MODEL), BF16)],
        compiler_params=_params(),
        name="kv_proj",
    )(mem, g, wkv)


def _xattn_kernel(x_ref, g_ref, wq_ref, kt_ref, v_ref, wo_ref, o_ref, x0, x1, y0, y1, *, n_tiles):
    _two_phase(pl.program_id(0), n_tiles,
               functools.partial(_attend, x_ref, g_ref, wq_ref, kt_ref, v_ref),
               functools.partial(_project_out_pieces, wo_ref, o_ref),
               ((x0, y0), (x1, y1)))


def _attend(x_ref, g_ref, wq_ref, kt_ref, v_ref, slot, pieces):
    x_scr, o_scr = slot
    pieces.run(2)
    x = x_ref[...]
    x_scr[...] = x
    n = _rms(x, g_ref[...]).astype(BF16)
    q = (jnp.dot(n, wq_ref[...], preferred_element_type=F32) * (XHEAD_DIM ** -0.5)).astype(BF16)

    def scores(h):
        return jnp.dot(q[:, h * XHEAD_DIM:(h + 1) * XHEAD_DIM], kt_ref[h],
                       preferred_element_type=F32)

    s_next = scores(0)
    for h in range(N_XHEADS):
        cols = slice(h * XHEAD_DIM, (h + 1) * XHEAD_DIM)
        s = s_next
        if h + 1 < N_XHEADS:
            s_next = scores(h + 1)
        pieces.run(2 if h < 2 else 1)
        p = jnp.exp(s - jnp.max(s, axis=-1, keepdims=True))
        attn = (p * (1.0 / jnp.sum(p, axis=-1, keepdims=True))).astype(BF16)
        o_scr[:, cols] = jnp.dot(attn, v_ref[:, cols], preferred_element_type=F32).astype(BF16)


def _xattn(x, st, g, wq, kt, v, wo):
    t = st.t
    tile = lambda j: jnp.minimum(j, st.n_tiles - 1)
    seq = lambda j: st.seq_pos(tile(j))[0]
    return pl.pallas_call(
        functools.partial(_xattn_kernel, n_tiles=st.n_tiles),
        grid=(st.n_tiles + 1,),
        in_specs=[pl.BlockSpec((t, D_MODEL), lambda j: (tile(j), 0)),
                  _resident(g.shape), _resident(wq.shape),
                  pl.BlockSpec((None, N_XHEADS, XHEAD_DIM, N_MEM), lambda j: (seq(j), 0, 0, 0)),
                  pl.BlockSpec((None, N_MEM, D_MODEL), lambda j: (seq(j), 0, 0)),
                  _resident(wo.shape)],
        out_specs=pl.BlockSpec((t, D_MODEL), lambda j: (jnp.maximum(j - 1, 0), 0)),
        out_shape=jax.ShapeDtypeStruct(x.shape, F32),
        scratch_shapes=_slot_scratch(t),
        compiler_params=_params(),
        name="xattn",
    )(x, g, wq, kt, v, wo)


def kernel(x_prompt, x_sample, mem_prompt, mem_sample, norm_ffn1, ffn1_w_gu, ffn1_w_down, norm_mix, w_in, conv_w, gmlp_w_s, gmlp_b_s, gmlp_ln_g, gmlp_ln_b, norm_conv_out, norm_gmlp_out, w_out, norm_mem, norm_cross, xattn_w_q, xattn_w_kv, xattn_w_o, norm_ffn2, ffn2_w_gu, ffn2_w_down, norm_final):
    assert ffn1_w_gu.shape[0] == 1, "single layer"
    row = lambda a: a.reshape(1, -1).astype(F32)
    bf = lambda a: a[0].astype(BF16)
    ws = gmlp_w_s[0].reshape(N_GMLP_HEADS // 2, 2 * CHUNK, CHUNK).astype(BF16)
    bs = jnp.repeat(gmlp_b_s[0].T, D_GMLP // N_GMLP_HEADS, axis=1).astype(F32)
    g_fin = row(norm_final)
    ffn1 = (row(norm_ffn1[0]), bf(ffn1_w_gu), bf(ffn1_w_down), g_fin)
    ffn2 = (row(norm_ffn2[0]), bf(ffn2_w_gu), bf(ffn2_w_down), g_fin)
    mix = (row(norm_mix[0]), bf(w_in), conv_w[0].astype(F32), ws, bs, row(gmlp_ln_g[0]),
           row(gmlp_ln_b[0]), row(norm_conv_out[0]), row(norm_gmlp_out[0]), bf(w_out))

    x1 = _ffn_first(x_prompt, x_sample, ffn1)
    x2 = _mix(x1, _Stream(x_prompt.shape, x_sample.shape, MIX_TILE), mix)
    mem = jnp.concatenate([mem_prompt, mem_sample], axis=0)
    kt, v = _kv(mem, row(norm_mem[0]), bf(xattn_w_kv))
    x3 = _xattn(x2, _Stream(x_prompt.shape, x_sample.shape, XATTN_TILE), row(norm_cross[0]),
                bf(xattn_w_q), kt, v, bf(xattn_w_o))
    st = _Stream(x_prompt.shape, x_sample.shape, FFN_TILE)
    y_prompt = _ffn_last(x3, 0, st.tiles_a, ffn2).reshape(x_prompt.shape)
    y_sample = _ffn_last(x3, st.tiles_a, st.tiles_b, ffn2).reshape(x_sample.shape)
    return y_prompt, y_sample
```

```python
import functools

import jax
import jax.numpy as jnp
from jax import lax
from jax.experimental import pallas as pl
from jax.experimental.pallas import tpu as pltpu

F32 = jnp.float32
BF16 = jnp.bfloat16

EPS = 1e-6
D_MODEL = 1024
D_FF = 2816
D_CONV = 512
D_GMLP = 512
N_GMLP_HEADS = 8
CHUNK = 128
N_MEM = 256
N_XHEADS = 4
XHEAD_DIM = D_MODEL // N_XHEADS

V7X_MXU_DIM = 256
V7X_BF16_SUBLANES = 16
V7X_VMEM_BYTES = 64 << 20
V7X_VREG_ELEMS = 8 * 128
V7X_BLOCK_VREGS = 16

FF_CHUNK = V7X_MXU_DIM
N_FF_CHUNKS = D_FF // FF_CHUNK
assert N_FF_CHUNKS * FF_CHUNK == D_FF
FFN_TILE = 1024
FFN_DOT_ROWS = 512
MIX_TILE = 512
XATTN_TILE = 1024
HALO = V7X_BF16_SUBLANES
VMEM_LIMIT = V7X_VMEM_BYTES - (8 << 20)


def _rms(x, g):
    ms = jnp.mean(x * x, axis=-1, keepdims=True)
    return x * lax.rsqrt(ms + EPS) * g


def _row_blocks(n_rows, width):
    rb = V7X_BLOCK_VREGS * V7X_VREG_ELEMS // width
    assert n_rows % rb == 0
    return [slice(r, r + rb) for r in range(0, n_rows, rb)]


def _resident(shape):
    zeros = (0,) * len(shape)
    return pl.BlockSpec(shape, lambda *_: zeros, pipeline_mode=pl.Buffered(1))


def _params():
    return pltpu.CompilerParams(vmem_limit_bytes=VMEM_LIMIT)


def _slot_scratch(t):
    return [pltpu.VMEM((t, D_MODEL), F32), pltpu.VMEM((t, D_MODEL), F32),
            pltpu.VMEM((t, D_MODEL), BF16), pltpu.VMEM((t, D_MODEL), BF16)]


class _Stream:
    def __init__(self, shape_a, shape_b, t):
        (ba, sa, _), (bb, sb, _) = shape_a, shape_b
        assert sa % t == 0 and sb % t == 0 and t % CHUNK == 0
        self.t = t
        self.tiles_a, self.tiles_b = ba * sa // t, bb * sb // t
        self.per_a, self.per_b = sa // t, sb // t
        self.seqs_a = ba
        self.n_tiles = self.tiles_a + self.tiles_b
        self.n_tok = self.n_tiles * t

    def seq_pos(self, j):
        in_a = j < self.tiles_a
        k = jnp.where(in_a, j, j - self.tiles_a)
        per = jnp.where(in_a, self.per_a, self.per_b)
        seq = jnp.where(in_a, k // self.per_a, self.seqs_a + k // self.per_b)
        return seq, k % per, per


def _ffn_kernel(*refs, tiles_a, final_norm):
    *x_refs, g_ref, wgu_ref, wd_ref, gfin_ref, o_ref = refs
    if len(x_refs) == 2:
        x = jnp.where(pl.program_id(0) < tiles_a, x_refs[0][...], x_refs[1][...])
    else:
        x = x_refs[0][...]
    n = _rms(x, g_ref[...]).astype(BF16)
    accs = [None] * (FFN_TILE // FFN_DOT_ROWS)
    for c in range(N_FF_CHUNKS):
        for r in range(len(accs)):
            nr = n[r * FFN_DOT_ROWS:(r + 1) * FFN_DOT_ROWS]
            gate = jnp.dot(nr, wgu_ref[:, c * FF_CHUNK:(c + 1) * FF_CHUNK],
                           preferred_element_type=F32)
            up = jnp.dot(nr, wgu_ref[:, D_FF + c * FF_CHUNK:D_FF + (c + 1) * FF_CHUNK],
                         preferred_element_type=F32)
            a = (jax.nn.silu(gate) * up).astype(BF16)
            d = jnp.dot(a, wd_ref[c * FF_CHUNK:(c + 1) * FF_CHUNK, :],
                        preferred_element_type=F32)
            accs[r] = d if accs[r] is None else accs[r] + d
    y = x + 0.5 * jnp.concatenate(accs, axis=0)
    if final_norm:
        y = _rms(y, gfin_ref[...])
    o_ref[...] = y


def _ffn_call(x_arrays, x_specs, n_tiles, tiles_a, consts, final_norm):
    t = FFN_TILE
    return pl.pallas_call(
        functools.partial(_ffn_kernel, tiles_a=tiles_a, final_norm=final_norm),
        grid=(n_tiles,),
        in_specs=list(x_specs) + [_resident(a.shape) for a in consts],
        out_specs=pl.BlockSpec((t, D_MODEL), lambda j: (j, 0)),
        out_shape=jax.ShapeDtypeStruct((n_tiles * t, D_MODEL), F32),
        compiler_params=_params(),
        name="ffn_final" if final_norm else "ffn",
    )(*x_arrays, *consts)


def _ffn_first(xa, xb, consts):
    t = FFN_TILE
    st = _Stream(xa.shape, xb.shape, t)
    ta = st.tiles_a
    specs = [pl.BlockSpec((t, D_MODEL), lambda j: (jnp.minimum(j, ta - 1), 0)),
             pl.BlockSpec((t, D_MODEL), lambda j: (jnp.maximum(j - ta, 0), 0))]
    arrays = [xa.reshape(-1, D_MODEL), xb.reshape(-1, D_MODEL)]
    return _ffn_call(arrays, specs, st.n_tiles, ta, consts, False)


def _ffn_last(x, first_tile, n_tiles, consts):
    spec = pl.BlockSpec((FFN_TILE, D_MODEL), lambda j: (j + first_tile, 0))
    return _ffn_call([x], [spec], n_tiles, 0, consts, True)


class _Pieces:
    def __init__(self, thunks=()):
        self._thunks = list(thunks)

    def run(self, k=None):
        k = len(self._thunks) if k is None else min(k, len(self._thunks))
        for thunk in self._thunks[:k]:
            thunk()
        del self._thunks[:k]


def _two_phase(j, n_tiles, first, second_pieces, slots):
    @pl.when(j == 0)
    def _():
        first(slots[0], _Pieces())

    for parity in (0, 1):
        @pl.when(jnp.logical_and(jnp.logical_and(j > 0, j < n_tiles), j % 2 == parity))
        def _():
            pieces = second_pieces(slots[1 - parity])
            first(slots[parity], pieces)
            pieces.run()

    @pl.when(j == n_tiles)
    def _():
        second_pieces(slots[(n_tiles - 1) % 2]).run()


def _project_out_pieces(w_ref, o_ref, slot):
    x_scr, y_scr = slot
    half = x_scr.shape[0] // 2

    def piece(k, r):
        rows = slice(r * half, (r + 1) * half)
        cols = slice(k * V7X_MXU_DIM, (k + 1) * V7X_MXU_DIM)
        o_ref[rows, cols] = x_scr[rows, cols] + jnp.dot(y_scr[rows, :], w_ref[:, cols],
                                                        preferred_element_type=F32)

    return _Pieces(functools.partial(piece, k, r)
                   for k in range(D_MODEL // V7X_MXU_DIM) for r in range(2))


def _mix_kernel(x_ref, xp_ref, xn_ref, g_ref, win_ref, cw_ref, ws_ref, bs_ref, lng_ref,
                lnb_ref, gco_ref, ggo_ref, wout_ref, o_ref, next_scr, puv_scr, pcx_scr, pb_scr,
                gu_scr, vln_scr, z_scr, mixed_scr, x0, x1, y0, y1, *, st):
    j = pl.program_id(0)
    scratch = (next_scr, puv_scr, pcx_scr, pb_scr, gu_scr, vln_scr, z_scr, mixed_scr)
    _two_phase(j, st.n_tiles,
               functools.partial(_mix_first, x_ref, xp_ref, xn_ref, g_ref, win_ref, cw_ref, ws_ref,
                                 bs_ref, lng_ref, lnb_ref, gco_ref, ggo_ref, scratch, st=st, j=j),
               functools.partial(_project_out_pieces, wout_ref, o_ref),
               ((x0, y0), (x1, y1)))


def _mix_first(x_ref, xp_ref, xn_ref, g_ref, win_ref, cw_ref, ws_ref, bs_ref, lng_ref, lnb_ref,
               gco_ref, ggo_ref, scratch, slot, pieces, *, st, j):
    t = st.t
    next_scr, puv_scr, pcx_scr, pb_scr, gu_scr, vln_scr, z_scr, mixed_scr = scratch
    x_scr, ymix_scr = slot
    _, pos, per = st.seq_pos(j)
    pieces.run(3)

    g = g_ref[...]
    next_scr[0:HALO, :] = _rms(xp_ref[...], g).astype(BF16)
    for rows in _row_blocks(t, D_MODEL):
        x = x_ref[rows, :]
        x_scr[rows, :] = x
        next_scr[HALO + rows.start:HALO + rows.stop, :] = _rms(x, g).astype(BF16)
    next_scr[HALO + t:, :] = _rms(xn_ref[...], g).astype(BF16)
    n_ref = next_scr.at[HALO:HALO + t, :]

    puv_scr[...] = jnp.dot(n_ref[...], win_ref[:, 3 * D_CONV:], preferred_element_type=F32)
    lng, lnb = lng_ref[...], lnb_ref[...]
    for rows in _row_blocks(t, D_GMLP):
        gu_scr[rows, :] = jax.nn.gelu(puv_scr[rows, :D_GMLP])
        gv = jax.nn.gelu(puv_scr[rows, D_GMLP:])
        vc = gv - jnp.mean(gv, axis=-1, keepdims=True)
        vln_scr[rows, :] = (vc * lax.rsqrt(jnp.mean(vc * vc, axis=-1, keepdims=True) + EPS) * lng
                            + lnb).astype(BF16)

    pcx_scr[...] = jnp.dot(next_scr[...], win_ref[:, D_CONV:3 * D_CONV],
                           preferred_element_type=F32)
    for rows in _row_blocks(t + 2 * HALO, D_CONV):
        z = pcx_scr[rows, :D_CONV] * pcx_scr[rows, D_CONV:]
        row_id = rows.start + lax.broadcasted_iota(jnp.int32, z.shape, 0)
        if rows.start < HALO:
            z = jnp.where(jnp.logical_and(pos == 0, row_id < HALO), 0.0, z)
        if rows.stop > HALO + t:
            z = jnp.where(jnp.logical_and(pos == per - 1, row_id >= HALO + t), 0.0, z)
        z_scr[rows, :] = z
    pb_scr[...] = jnp.dot(n_ref[...], win_ref[:, 0:D_CONV], preferred_element_type=F32)
    cw = cw_ref[...]
    gco = gco_ref[...]
    for rows in _row_blocks(t, D_CONV):
        lo, hi = HALO + rows.start, HALO + rows.stop
        conv = (cw[0:1] * z_scr[lo - 1:hi - 1, :] + cw[1:2] * z_scr[lo:hi, :]
                + cw[2:3] * z_scr[lo + 1:hi + 1, :])
        ymix_scr[rows, 0:D_CONV] = _rms(pb_scr[rows, :] * conv, gco).astype(BF16)

    first_head = lax.broadcasted_iota(jnp.int32, (CHUNK, 128), 1) < (D_GMLP // N_GMLP_HEADS)
    ggo = ggo_ref[...]
    for c in range(t // CHUNK):
        if c % 2 == 0:
            pieces.run(1)
        crows = slice(c * CHUNK, (c + 1) * CHUNK)
        for hp in range(N_GMLP_HEADS // 2):
            lanes = slice(hp * 128, (hp + 1) * 128)
            r = jnp.dot(ws_ref[hp], vln_scr[crows, lanes], preferred_element_type=F32)
            mixed_scr[crows, lanes] = (jnp.where(first_head, r[:CHUNK], r[CHUNK:])
                                       + bs_ref[:, lanes])
        for rows in _row_blocks(CHUNK, D_GMLP):
            rr = slice(crows.start + rows.start, crows.start + rows.stop)
            ymix_scr[rr, D_CONV:] = _rms(gu_scr[rr, :] * mixed_scr[rr, :], ggo).astype(BF16)


def _mix(x, st, consts):
    t = st.t
    per_tile = t // HALO
    last_halo = st.n_tok // HALO - 1
    tile = lambda j: jnp.minimum(j, st.n_tiles - 1)
    row = pl.BlockSpec((t, D_MODEL), lambda j: (tile(j), 0))
    prev = pl.BlockSpec((HALO, D_MODEL), lambda j: (jnp.maximum(tile(j) * per_tile - 1, 0), 0))
    nxt = pl.BlockSpec((HALO, D_MODEL),
                       lambda j: (jnp.minimum((tile(j) + 1) * per_tile, last_halo), 0))
    return pl.pallas_call(
        functools.partial(_mix_kernel, st=st),
        grid=(st.n_tiles + 1,),
        in_specs=[row, prev, nxt] + [_resident(a.shape) for a in consts],
        out_specs=pl.BlockSpec((t, D_MODEL), lambda j: (jnp.maximum(j - 1, 0), 0)),
        out_shape=jax.ShapeDtypeStruct(x.shape, F32),
        scratch_shapes=[pltpu.VMEM((t + 2 * HALO, D_MODEL), BF16),
                        pltpu.VMEM((t, 2 * D_GMLP), F32),
                        pltpu.VMEM((t + 2 * HALO, 2 * D_CONV), F32),
                        pltpu.VMEM((t, D_CONV), F32),
                        pltpu.VMEM((t, D_GMLP), F32),
                        pltpu.VMEM((t, D_GMLP), BF16),
                        pltpu.VMEM((t + 2 * HALO, D_CONV), F32),
                        pltpu.VMEM((t, D_GMLP), F32)]
                       + _slot_scratch(t),
        compiler_params=_params(),
        name="mix",
    )(x, x, x, *consts)


def _kv_kernel(mem_a_ref, mem_b_ref, g_ref, wkv_ref, kt_ref, v_ref, *, seqs_a):
    mem = jnp.where(pl.program_id(0) < seqs_a, mem_a_ref[...], mem_b_ref[...])
    n = _rms(mem, g_ref[...]).astype(BF16)
    kv = jnp.dot(n, wkv_ref[...], preferred_element_type=F32)
    for h in range(N_XHEADS):
        kt_ref[h] = kv[:, h * XHEAD_DIM:(h + 1) * XHEAD_DIM].T.astype(BF16)
    v_ref[...] = kv[:, D_MODEL:].astype(BF16)


def _kv(mem_a, mem_b, g, wkv):
    ba = mem_a.shape[0]
    b = ba + mem_b.shape[0]
    return pl.pallas_call(
        functools.partial(_kv_kernel, seqs_a=ba),
        grid=(b,),
        in_specs=[pl.BlockSpec((None, N_MEM, D_MODEL), lambda bi: (jnp.minimum(bi, ba - 1), 0, 0)),
                  pl.BlockSpec((None, N_MEM, D_MODEL), lambda bi: (jnp.maximum(bi - ba, 0), 0, 0)),
                  _resident(g.shape), _resident(wkv.shape)],
        out_specs=[pl.BlockSpec((None, N_XHEADS, XHEAD_DIM, N_MEM), lambda bi: (bi, 0, 0, 0)),
                   pl.BlockSpec((None, N_MEM, D_MODEL), lambda bi: (bi, 0, 0))],
        out_shape=[jax.ShapeDtypeStruct((b, N_XHEADS, XHEAD_DIM, N_MEM), BF16),
                   jax.ShapeDtypeStruct((b, N_MEM, D_MODEL), BF16)],
        compiler_params=_params(),
        name="kv_proj",
    )(mem_a, mem_b, g, wkv)


def _xattn_kernel(x_ref, g_ref, wq_ref, kt_ref, v_ref, wo_ref, o_ref, n_scr, q_scr, qb_scr,
                  s_scr, a_scr, x0, x1, y0, y1, *, n_tiles):
    scratch = (n_scr, q_scr, qb_scr, s_scr, a_scr)
    _two_phase(pl.program_id(0), n_tiles,
               functools.partial(_attend, x_ref, g_ref, wq_ref, kt_ref, v_ref, scratch),
               functools.partial(_project_out_pieces, wo_ref, o_ref),
               ((x0, y0), (x1, y1)))


def _attend(x_ref, g_ref, wq_ref, kt_ref, v_ref, scratch, slot, pieces):
    n_scr, q_scr, qb_scr, s_scr, a_scr = scratch
    x_scr, o_scr = slot
    t = x_scr.shape[0]
    pieces.run(2)
    g = g_ref[...]
    for rows in _row_blocks(t, D_MODEL):
        x = x_ref[rows, :]
        x_scr[rows, :] = x
        n_scr[rows, :] = _rms(x, g).astype(BF16)
    q_scr[...] = jnp.dot(n_scr[...], wq_ref[...], preferred_element_type=F32)
    for rows in _row_blocks(t, D_MODEL):
        qb_scr[rows, :] = (q_scr[rows, :] * (XHEAD_DIM ** -0.5)).astype(BF16)

    def scores(h):
        s_scr[h] = jnp.dot(qb_scr[:, h * XHEAD_DIM:(h + 1) * XHEAD_DIM], kt_ref[h],
                           preferred_element_type=F32)

    scores(0)
    for h in range(N_XHEADS):
        cols = slice(h * XHEAD_DIM, (h + 1) * XHEAD_DIM)
        if h + 1 < N_XHEADS:
            scores(h + 1)
        pieces.run(2 if h < 2 else 1)
        for rows in _row_blocks(t, N_MEM):
            s = s_scr[h, rows, :]
            p = jnp.exp(s - jnp.max(s, axis=-1, keepdims=True))
            a_scr[h, rows, :] = (p * (1.0 / jnp.sum(p, axis=-1, keepdims=True))).astype(BF16)
        o_scr[:, cols] = jnp.dot(a_scr[h], v_ref[:, cols], preferred_element_type=F32).astype(BF16)


def _xattn(x, st, g, wq, kt, v, wo):
    t = st.t
    tile = lambda j: jnp.minimum(j, st.n_tiles - 1)
    seq = lambda j: st.seq_pos(tile(j))[0]
    return pl.pallas_call(
        functools.partial(_xattn_kernel, n_tiles=st.n_tiles),
        grid=(st.n_tiles + 1,),
        in_specs=[pl.BlockSpec((t, D_MODEL), lambda j: (tile(j), 0)),
                  _resident(g.shape), _resident(wq.shape),
                  pl.BlockSpec((None, N_XHEADS, XHEAD_DIM, N_MEM), lambda j: (seq(j), 0, 0, 0)),
                  pl.BlockSpec((None, N_MEM, D_MODEL), lambda j: (seq(j), 0, 0)),
                  _resident(wo.shape)],
        out_specs=pl.BlockSpec((t, D_MODEL), lambda j: (jnp.maximum(j - 1, 0), 0)),
        out_shape=jax.ShapeDtypeStruct(x.shape, F32),
        scratch_shapes=[pltpu.VMEM((t, D_MODEL), BF16),
                        pltpu.VMEM((t, D_MODEL), F32),
                        pltpu.VMEM((t, D_MODEL), BF16),
                        pltpu.VMEM((N_XHEADS, t, N_MEM), F32),
                        pltpu.VMEM((N_XHEADS, t, N_MEM), BF16)]
                       + _slot_scratch(t),
        compiler_params=_params(),
        name="xattn",
    )(x, g, wq, kt, v, wo)


def kernel(x_prompt, x_sample, mem_prompt, mem_sample, norm_ffn1, ffn1_w_gu, ffn1_w_down, norm_mix, w_in, conv_w, gmlp_w_s, gmlp_b_s, gmlp_ln_g, gmlp_ln_b, norm_conv_out, norm_gmlp_out, w_out, norm_mem, norm_cross, xattn_w_q, xattn_w_kv, xattn_w_o, norm_ffn2, ffn2_w_gu, ffn2_w_down, norm_final):
    assert ffn1_w_gu.shape[0] == 1, "single layer"
    row = lambda a: a.reshape(1, -1).astype(F32)
    bf = lambda a: a[0].astype(BF16)
    ws = gmlp_w_s[0].reshape(N_GMLP_HEADS // 2, 2 * CHUNK, CHUNK).astype(BF16)
    bs = jnp.repeat(gmlp_b_s[0].T, D_GMLP // N_GMLP_HEADS, axis=1).astype(F32)
    g_fin = row(norm_final)
    ffn1 = (row(norm_ffn1[0]), bf(ffn1_w_gu), bf(ffn1_w_down), g_fin)
    ffn2 = (row(norm_ffn2[0]), bf(ffn2_w_gu), bf(ffn2_w_down), g_fin)
    mix = (row(norm_mix[0]), bf(w_in), conv_w[0].astype(F32), ws, bs, row(gmlp_ln_g[0]),
           row(gmlp_ln_b[0]), row(norm_conv_out[0]), row(norm_gmlp_out[0]), bf(w_out))

    x1 = _ffn_first(x_prompt, x_sample, ffn1)
    x2 = _mix(x1, _Stream(x_prompt.shape, x_sample.shape, MIX_TILE), mix)
    kt, v = _kv(mem_prompt, mem_sample, row(norm_mem[0]), bf(xattn_w_kv))
    x3 = _xattn(x2, _Stream(x_prompt.shape, x_sample.shape, XATTN_TILE), row(norm_cross[0]),
                bf(xattn_w_q), kt, v, bf(xattn_w_o))
    st = _Stream(x_prompt.shape, x_sample.shape, FFN_TILE)
    y_prompt = _ffn_last(x3, 0, st.tiles_a, ffn2).reshape(x_prompt.shape)
    y_sample = _ffn_last(x3, st.tiles_a, st.tiles_b, ffn2).reshape(x_sample.shape)
    return y_prompt, y_sample
```

```python
import functools

import jax
import jax.numpy as jnp
from jax import lax
from jax.experimental import pallas as pl
from jax.experimental.pallas import tpu as pltpu

F32 = jnp.float32
BF16 = jnp.bfloat16

EPS = 1e-6
D_MODEL = 1024
D_FF = 2816
D_CONV = 512
D_GMLP = 512
N_GMLP_HEADS = 8
CHUNK = 128
N_MEM = 256
N_XHEADS = 4
XHEAD_DIM = D_MODEL // N_XHEADS

V7X_MXU_DIM = 256
V7X_BF16_SUBLANES = 16
V7X_VMEM_BYTES = 64 << 20

FF_CHUNK = V7X_MXU_DIM
N_FF_CHUNKS = D_FF // FF_CHUNK
assert N_FF_CHUNKS * FF_CHUNK == D_FF
FFN_TILE = 1024
MIX_TILE = 1024
XATTN_TILE = 1024
HALO = V7X_BF16_SUBLANES
VMEM_LIMIT = V7X_VMEM_BYTES - (8 << 20)


def _rms(x, g):
    ms = jnp.mean(x * x, axis=-1, keepdims=True)
    return x * lax.rsqrt(ms + EPS) * g


def _resident(shape):
    zeros = (0,) * len(shape)
    return pl.BlockSpec(shape, lambda *_: zeros, pipeline_mode=pl.Buffered(1))


def _params():
    return pltpu.CompilerParams(vmem_limit_bytes=VMEM_LIMIT)


def _slot_scratch(t):
    return [pltpu.VMEM((t, D_MODEL), F32), pltpu.VMEM((t, D_MODEL), F32),
            pltpu.VMEM((t, D_MODEL), BF16), pltpu.VMEM((t, D_MODEL), BF16)]


class _Stream:
    def __init__(self, shape_a, shape_b, t):
        (ba, sa, _), (bb, sb, _) = shape_a, shape_b
        assert sa % t == 0 and sb % t == 0 and t % CHUNK == 0
        self.t = t
        self.tiles_a, self.tiles_b = ba * sa // t, bb * sb // t
        self.per_a, self.per_b = sa // t, sb // t
        self.seqs_a = ba
        self.n_tiles = self.tiles_a + self.tiles_b
        self.n_tok = self.n_tiles * t

    def seq_pos(self, j):
        in_a = j < self.tiles_a
        k = jnp.where(in_a, j, j - self.tiles_a)
        per = jnp.where(in_a, self.per_a, self.per_b)
        seq = jnp.where(in_a, k // self.per_a, self.seqs_a + k // self.per_b)
        return seq, k % per, per


def _ffn_kernel(*refs, tiles_a, final_norm):
    *x_refs, g_ref, wgu_ref, wd_ref, gfin_ref, o_ref = refs
    if len(x_refs) == 2:
        x = jnp.where(pl.program_id(0) < tiles_a, x_refs[0][...], x_refs[1][...])
    else:
        x = x_refs[0][...]
    n = _rms(x, g_ref[...]).astype(BF16)
    n_parts = 2 if final_norm else 1
    part = FFN_TILE // n_parts
    accs = [None] * n_parts
    for c in range(N_FF_CHUNKS):
        for r in range(n_parts):
            nr = n[r * part:(r + 1) * part]
            gate = jnp.dot(nr, wgu_ref[:, c * FF_CHUNK:(c + 1) * FF_CHUNK],
                           preferred_element_type=F32)
            up = jnp.dot(nr, wgu_ref[:, D_FF + c * FF_CHUNK:D_FF + (c + 1) * FF_CHUNK],
                         preferred_element_type=F32)
            a = (jax.nn.silu(gate) * up).astype(BF16)
            d = jnp.dot(a, wd_ref[c * FF_CHUNK:(c + 1) * FF_CHUNK, :],
                        preferred_element_type=F32)
            accs[r] = d if accs[r] is None else accs[r] + d
    y = x + 0.5 * jnp.concatenate(accs, axis=0)
    if final_norm:
        y = _rms(y, gfin_ref[...])
    o_ref[...] = y


def _ffn_call(x_arrays, x_specs, n_tiles, tiles_a, consts, final_norm):
    t = FFN_TILE
    return pl.pallas_call(
        functools.partial(_ffn_kernel, tiles_a=tiles_a, final_norm=final_norm),
        grid=(n_tiles,),
        in_specs=list(x_specs) + [_resident(a.shape) for a in consts],
        out_specs=pl.BlockSpec((t, D_MODEL), lambda j: (j, 0)),
        out_shape=jax.ShapeDtypeStruct((n_tiles * t, D_MODEL), F32),
        compiler_params=_params(),
        name="ffn_final" if final_norm else "ffn",
    )(*x_arrays, *consts)


def _ffn_first(xa, xb, consts):
    t = FFN_TILE
    st = _Stream(xa.shape, xb.shape, t)
    ta = st.tiles_a
    specs = [pl.BlockSpec((t, D_MODEL), lambda j: (jnp.minimum(j, ta - 1), 0)),
             pl.BlockSpec((t, D_MODEL), lambda j: (jnp.maximum(j - ta, 0), 0))]
    arrays = [xa.reshape(-1, D_MODEL), xb.reshape(-1, D_MODEL)]
    return _ffn_call(arrays, specs, st.n_tiles, ta, consts, False)


def _ffn_last(x, first_tile, n_tiles, consts):
    spec = pl.BlockSpec((FFN_TILE, D_MODEL), lambda j: (j + first_tile, 0))
    return _ffn_call([x], [spec], n_tiles, 0, consts, True)


class _Pieces:
    def __init__(self, thunks=()):
        self._thunks = list(thunks)

    def run(self, k=None):
        k = len(self._thunks) if k is None else min(k, len(self._thunks))
        for thunk in self._thunks[:k]:
            thunk()
        del self._thunks[:k]


def _two_phase(j, n_tiles, first, second_pieces, slots):
    @pl.when(j == 0)
    def _():
        first(slots[0], _Pieces())

    for parity in (0, 1):
        @pl.when(jnp.logical_and(jnp.logical_and(j > 0, j < n_tiles), j % 2 == parity))
        def _():
            pieces = second_pieces(slots[1 - parity])
            first(slots[parity], pieces)
            pieces.run()

    @pl.when(j == n_tiles)
    def _():
        second_pieces(slots[(n_tiles - 1) % 2]).run()


def _project_out_pieces(w_ref, o_ref, slot):
    x_scr, y_scr = slot
    half = x_scr.shape[0] // 2

    def piece(k, r):
        rows = slice(r * half, (r + 1) * half)
        cols = slice(k * V7X_MXU_DIM, (k + 1) * V7X_MXU_DIM)
        o_ref[rows, cols] = x_scr[rows, cols] + jnp.dot(y_scr[rows, :], w_ref[:, cols],
                                                        preferred_element_type=F32)

    return _Pieces(functools.partial(piece, k, r)
                   for k in range(D_MODEL // V7X_MXU_DIM) for r in range(2))


def _mix_kernel(x_ref, xp_ref, xn_ref, g_ref, win_ref, cw_ref, ws_ref, bs_ref, lng_ref,
                lnb_ref, gco_ref, ggo_ref, wout_ref, o_ref, next_scr,
                x0, x1, y0, y1, *, st):
    j = pl.program_id(0)
    _two_phase(j, st.n_tiles,
               functools.partial(_mix_first, x_ref, xp_ref, xn_ref, g_ref, win_ref, cw_ref, ws_ref,
                                 bs_ref, lng_ref, lnb_ref, gco_ref, ggo_ref, next_scr,
                                 st=st, j=j),
               functools.partial(_project_out_pieces, wout_ref, o_ref),
               ((x0, y0), (x1, y1)))


def _mix_first(x_ref, xp_ref, xn_ref, g_ref, win_ref, cw_ref, ws_ref, bs_ref, lng_ref, lnb_ref,
               gco_ref, ggo_ref, next_scr, slot, pieces, *, st, j):
    t = st.t
    x_scr, ymix_scr = slot
    _, pos, per = st.seq_pos(j)
    x = x_ref[...]
    x_scr[...] = x
    g = g_ref[...]
    n = _rms(x, g).astype(BF16)
    next_scr[0:HALO, :] = _rms(xp_ref[...], g).astype(BF16)
    next_scr[HALO:HALO + t, :] = n
    next_scr[HALO + t:, :] = _rms(xn_ref[...], g).astype(BF16)
    pieces.run(3)

    half = t // 2
    puv = jnp.concatenate([jnp.dot(n[r * half:(r + 1) * half], win_ref[:, 3 * D_CONV:],
                                   preferred_element_type=F32) for r in range(2)],
                          axis=0)
    gu = jax.nn.gelu(puv[:, :D_GMLP])
    gv = jax.nn.gelu(puv[:, D_GMLP:])
    mu = jnp.mean(gv, axis=-1, keepdims=True)
    vc = gv - mu
    vln = (vc * lax.rsqrt(jnp.mean(vc * vc, axis=-1, keepdims=True) + EPS) * lng_ref[...]
           + lnb_ref[...]).astype(BF16)

    pcx = jnp.dot(next_scr[...], win_ref[:, D_CONV:3 * D_CONV], preferred_element_type=F32)
    z = pcx[:, :D_CONV] * pcx[:, D_CONV:]
    z = jnp.concatenate([jnp.where(pos == 0, 0.0, z[:HALO]), z[HALO:HALO + t],
                         jnp.where(pos == per - 1, 0.0, z[HALO + t:])], axis=0)
    rows_ext = t + 2 * HALO
    cw = cw_ref[...]
    conv = (cw[0:1] * pltpu.roll(z, 1, 0)[HALO:HALO + t] + cw[1:2] * z[HALO:HALO + t]
            + cw[2:3] * pltpu.roll(z, rows_ext - 1, 0)[HALO:HALO + t])
    pb = jnp.dot(n, win_ref[:, 0:D_CONV], preferred_element_type=F32)
    ymix_scr[:, 0:D_CONV] = _rms(pb * conv, gco_ref[...]).astype(BF16)

    first_head = lax.broadcasted_iota(jnp.int32, (CHUNK, 128), 1) < (D_GMLP // N_GMLP_HEADS)
    chunks = []
    n_chunks = t // CHUNK
    for c in range(n_chunks):
        if c % 2 == 0 and c < 6:
            pieces.run(1)
        vck = vln[c * CHUNK:(c + 1) * CHUNK]
        pairs = []
        for hp in range(N_GMLP_HEADS // 2):
            r = jnp.dot(ws_ref[hp], vck[:, hp * 128:(hp + 1) * 128], preferred_element_type=F32)
            pairs.append(jnp.where(first_head, r[:CHUNK], r[CHUNK:]))
        chunks.append(jnp.concatenate(pairs, axis=1) + bs_ref[...])
    mixed = jnp.concatenate(chunks, axis=0)
    ymix_scr[:, D_CONV:] = _rms(gu * mixed, ggo_ref[...]).astype(BF16)


def _mix(x, st, consts):
    t = st.t
    per_tile = t // HALO
    last_halo = st.n_tok // HALO - 1
    tile = lambda j: jnp.minimum(j, st.n_tiles - 1)
    row = pl.BlockSpec((t, D_MODEL), lambda j: (tile(j), 0))
    prev = pl.BlockSpec((HALO, D_MODEL), lambda j: (jnp.maximum(tile(j) * per_tile - 1, 0), 0))
    nxt = pl.BlockSpec((HALO, D_MODEL),
                       lambda j: (jnp.minimum((tile(j) + 1) * per_tile, last_halo), 0))
    return pl.pallas_call(
        functools.partial(_mix_kernel, st=st),
        grid=(st.n_tiles + 1,),
        in_specs=[row, prev, nxt] + [_resident(a.shape) for a in consts],
        out_specs=pl.BlockSpec((t, D_MODEL), lambda j: (jnp.maximum(j - 1, 0), 0)),
        out_shape=jax.ShapeDtypeStruct(x.shape, F32),
        scratch_shapes=[pltpu.VMEM((t + 2 * HALO, D_MODEL), BF16)] + _slot_scratch(t),
        compiler_params=_params(),
        name="mix",
    )(x, x, x, *consts)


def _kv_kernel(mem_a_ref, mem_b_ref, g_ref, wkv_ref, kt_ref, v_ref, *, seqs_a):
    mem = jnp.where(pl.program_id(0) < seqs_a, mem_a_ref[...], mem_b_ref[...])
    n = _rms(mem, g_ref[...]).astype(BF16)
    kv = jnp.dot(n, wkv_ref[...], preferred_element_type=F32)
    for h in range(N_XHEADS):
        kt_ref[h] = kv[:, h * XHEAD_DIM:(h + 1) * XHEAD_DIM].T.astype(BF16)
    v_ref[...] = kv[:, D_MODEL:].astype(BF16)


def _kv(mem_a, mem_b, g, wkv):
    ba = mem_a.shape[0]
    b = ba + mem_b.shape[0]
    return pl.pallas_call(
        functools.partial(_kv_kernel, seqs_a=ba),
        grid=(b,),
        in_specs=[pl.BlockSpec((None, N_MEM, D_MODEL), lambda bi: (jnp.minimum(bi, ba - 1), 0, 0)),
                  pl.BlockSpec((None, N_MEM, D_MODEL), lambda bi: (jnp.maximum(bi - ba, 0), 0, 0)),
                  _resident(g.shape), _resident(wkv.shape)],
        out_specs=[pl.BlockSpec((None, N_XHEADS, XHEAD_DIM, N_MEM), lambda bi: (bi, 0, 0, 0)),
                   pl.BlockSpec((None, N_MEM, D_MODEL), lambda bi: (bi, 0, 0))],
        out_shape=[jax.ShapeDtypeStruct((b, N_XHEADS, XHEAD_DIM, N_MEM), BF16),
                   jax.ShapeDtypeStruct((b, N_MEM, D_MODEL), BF16)],
        compiler_params=_params(),
        name="kv_proj",
    )(mem_a, mem_b, g, wkv)


def _xattn_kernel(x_ref, g_ref, wq_ref, kt_ref, v_ref, wo_ref, o_ref, x0, x1, y0, y1, *, n_tiles):
    _two_phase(pl.program_id(0), n_tiles,
               functools.partial(_attend, x_ref, g_ref, wq_ref, kt_ref, v_ref),
               functools.partial(_project_out_pieces, wo_ref, o_ref),
               ((x0, y0), (x1, y1)))


def _attend(x_ref, g_ref, wq_ref, kt_ref, v_ref, slot, pieces):
    x_scr, o_scr = slot
    x = x_ref[...]
    x_scr[...] = x
    n = _rms(x, g_ref[...]).astype(BF16)
    pieces.run(2)
    half = n.shape[0] // 2
    q = jnp.concatenate([jnp.dot(n[r * half:(r + 1) * half], wq_ref[...],
                                 preferred_element_type=F32) for r in range(2)], axis=0)
    q = (q * (XHEAD_DIM ** -0.5)).astype(BF16)

    def scores(h):
        return jnp.dot(q[:, h * XHEAD_DIM:(h + 1) * XHEAD_DIM], kt_ref[h],
                       preferred_element_type=F32)

    s_next = scores(0)
    for h in range(N_XHEADS):
        cols = slice(h * XHEAD_DIM, (h + 1) * XHEAD_DIM)
        s = s_next
        if h + 1 < N_XHEADS:
            s_next = scores(h + 1)
        pieces.run(2 if h < 2 else 1)
        p = jnp.exp(s - jnp.max(s, axis=-1, keepdims=True))
        attn = (p * (1.0 / jnp.sum(p, axis=-1, keepdims=True))).astype(BF16)
        o_scr[:, cols] = jnp.dot(attn, v_ref[:, cols], preferred_element_type=F32).astype(BF16)


def _xattn(x, st, g, wq, kt, v, wo):
    t = st.t
    tile = lambda j: jnp.minimum(j, st.n_tiles - 1)
    seq = lambda j: st.seq_pos(tile(j))[0]
    return pl.pallas_call(
        functools.partial(_xattn_kernel, n_tiles=st.n_tiles),
        grid=(st.n_tiles + 1,),
        in_specs=[pl.BlockSpec((t, D_MODEL), lambda j: (tile(j), 0)),
                  _resident(g.shape), _resident(wq.shape),
                  pl.BlockSpec((None, N_XHEADS, XHEAD_DIM, N_MEM), lambda j: (seq(j), 0, 0, 0)),
                  pl.BlockSpec((None, N_MEM, D_MODEL), lambda j: (seq(j), 0, 0)),
                  _resident(wo.shape)],
        out_specs=pl.BlockSpec((t, D_MODEL), lambda j: (jnp.maximum(j - 1, 0), 0)),
        out_shape=jax.ShapeDtypeStruct(x.shape, F32),
        scratch_shapes=_slot_scratch(t),
        compiler_params=_params(),
        name="xattn",
    )(x, g, wq, kt, v, wo)


def kernel(x_prompt, x_sample, mem_prompt, mem_sample, norm_ffn1, ffn1_w_gu, ffn1_w_down, norm_mix, w_in, conv_w, gmlp_w_s, gmlp_b_s, gmlp_ln_g, gmlp_ln_b, norm_conv_out, norm_gmlp_out, w_out, norm_mem, norm_cross, xattn_w_q, xattn_w_kv, xattn_w_o, norm_ffn2, ffn2_w_gu, ffn2_w_down, norm_final):
    assert ffn1_w_gu.shape[0] == 1, "single layer"
    row = lambda a: a.reshape(1, -1).astype(F32)
    bf = lambda a: a[0].astype(BF16)
    ws = gmlp_w_s[0].reshape(N_GMLP_HEADS // 2, 2 * CHUNK, CHUNK).astype(BF16)
    bs = jnp.repeat(gmlp_b_s[0].T, D_GMLP // N_GMLP_HEADS, axis=1).astype(F32)
    g_fin = row(norm_final)
    ffn1 = (row(norm_ffn1[0]), bf(ffn1_w_gu), bf(ffn1_w_down), g_fin)
    ffn2 = (row(norm_ffn2[0]), bf(ffn2_w_gu), bf(ffn2_w_down), g_fin)
    mix = (row(norm_mix[0]), bf(w_in), conv_w[0].astype(F32), ws, bs, row(gmlp_ln_g[0]),
           row(gmlp_ln_b[0]), row(norm_conv_out[0]), row(norm_gmlp_out[0]), bf(w_out))

    x1 = _ffn_first(x_prompt, x_sample, ffn1)
    x2 = _mix(x1, _Stream(x_prompt.shape, x_sample.shape, MIX_TILE), mix)
    kt, v = _kv(mem_prompt, mem_sample, row(norm_mem[0]), bf(xattn_w_kv))
    x3 = _xattn(x2, _Stream(x_prompt.shape, x_sample.shape, XATTN_TILE), row(norm_cross[0]),
                bf(xattn_w_q), kt, v, bf(xattn_w_o))
    st = _Stream(x_prompt.shape, x_sample.shape, FFN_TILE)
    y_prompt = _ffn_last(x3, 0, st.tiles_a, ffn2).reshape(x_prompt.shape)
    y_sample = _ffn_last(x3, st.tiles_a, st.tiles_b, ffn2).reshape(x_sample.shape)
    return y_prompt, y_sample
```

```python
import functools

import jax
import jax.numpy as jnp
from jax import lax
from jax.experimental import pallas as pl
from jax.experimental.pallas import tpu as pltpu

F32 = jnp.float32
BF16 = jnp.bfloat16

EPS = 1e-6
D_MODEL = 1024
D_FF = 2816
D_CONV = 512
D_GMLP = 512
N_GMLP_HEADS = 8
CHUNK = 128
N_MEM = 256
N_XHEADS = 4
XHEAD_DIM = D_MODEL // N_XHEADS

V7X_MXU_DIM = 256
V7X_BF16_SUBLANES = 16
V7X_VMEM_BYTES = 64 << 20

FF_CHUNK = V7X_MXU_DIM
N_FF_CHUNKS = D_FF // FF_CHUNK
assert N_FF_CHUNKS * FF_CHUNK == D_FF
FFN_TILE = 1024
MIX_TILE = 1024
XATTN_TILE = 1024
HALO = V7X_BF16_SUBLANES
VMEM_LIMIT = V7X_VMEM_BYTES - (8 << 20)


def _rms(x, g=None):
    ms = jnp.mean(x * x, axis=-1, keepdims=True)
    y = x * lax.rsqrt(ms + EPS)
    return y if g is None else y * g


def _gelu(x):
    c = (2.0 / jnp.pi) ** 0.5
    h = 0.5 * x
    return h + h * jnp.tanh(x * (c + (0.044715 * c) * (x * x)))


def _fold_gain(g, w):
    return (g.astype(F32)[:, None] * w.astype(F32)).astype(BF16)


def _resident(shape):
    zeros = (0,) * len(shape)
    return pl.BlockSpec(shape, lambda *_: zeros, pipeline_mode=pl.Buffered(1))


def _params():
    return pltpu.CompilerParams(vmem_limit_bytes=VMEM_LIMIT)


def _slot_scratch(t):
    return [pltpu.VMEM((t, D_MODEL), F32), pltpu.VMEM((t, D_MODEL), F32),
            pltpu.VMEM((t, D_MODEL), BF16), pltpu.VMEM((t, D_MODEL), BF16)]


class _Stream:
    def __init__(self, shape_a, shape_b, t):
        (ba, sa, _), (bb, sb, _) = shape_a, shape_b
        assert sa % t == 0 and sb % t == 0 and t % CHUNK == 0
        self.t = t
        self.tiles_a, self.tiles_b = ba * sa // t, bb * sb // t
        self.per_a, self.per_b = sa // t, sb // t
        self.seqs_a = ba
        self.n_tiles = self.tiles_a + self.tiles_b
        self.n_tok = self.n_tiles * t

    def seq_pos(self, j):
        in_a = j < self.tiles_a
        k = jnp.where(in_a, j, j - self.tiles_a)
        per = jnp.where(in_a, self.per_a, self.per_b)
        seq = jnp.where(in_a, k // self.per_a, self.seqs_a + k // self.per_b)
        return seq, k % per, per


def _ffn_kernel(*refs, tiles_a, final_norm):
    *x_refs, wgu_ref, wd_ref, gfin_ref, o_ref = refs
    if len(x_refs) == 2:
        x = jnp.where(pl.program_id(0) < tiles_a, x_refs[0][...], x_refs[1][...])
    else:
        x = x_refs[0][...]
    n = _rms(x).astype(BF16)
    n_parts = 2 if final_norm else 1
    part = FFN_TILE // n_parts
    accs = [None] * n_parts
    for c in range(N_FF_CHUNKS):
        for r in range(n_parts):
            nr = n[r * part:(r + 1) * part]
            gate = jnp.dot(nr, wgu_ref[:, c * FF_CHUNK:(c + 1) * FF_CHUNK],
                           preferred_element_type=F32)
            up = jnp.dot(nr, wgu_ref[:, D_FF + c * FF_CHUNK:D_FF + (c + 1) * FF_CHUNK],
                         preferred_element_type=F32)
            a = (jax.nn.silu(gate) * up).astype(BF16)
            d = jnp.dot(a, wd_ref[c * FF_CHUNK:(c + 1) * FF_CHUNK, :],
                        preferred_element_type=F32)
            accs[r] = d if accs[r] is None else accs[r] + d
    y = x + 0.5 * jnp.concatenate(accs, axis=0)
    if final_norm:
        y = _rms(y, gfin_ref[...])
    o_ref[...] = y


def _ffn_call(x_arrays, x_specs, n_tiles, tiles_a, consts, final_norm):
    t = FFN_TILE
    return pl.pallas_call(
        functools.partial(_ffn_kernel, tiles_a=tiles_a, final_norm=final_norm),
        grid=(n_tiles,),
        in_specs=list(x_specs) + [_resident(a.shape) for a in consts],
        out_specs=pl.BlockSpec((t, D_MODEL), lambda j: (j, 0)),
        out_shape=jax.ShapeDtypeStruct((n_tiles * t, D_MODEL), F32),
        compiler_params=_params(),
        name="ffn_final" if final_norm else "ffn",
    )(*x_arrays, *consts)


def _ffn_first(xa, xb, consts):
    t = FFN_TILE
    st = _Stream(xa.shape, xb.shape, t)
    ta = st.tiles_a
    specs = [pl.BlockSpec((t, D_MODEL), lambda j: (jnp.minimum(j, ta - 1), 0)),
             pl.BlockSpec((t, D_MODEL), lambda j: (jnp.maximum(j - ta, 0), 0))]
    arrays = [xa.reshape(-1, D_MODEL), xb.reshape(-1, D_MODEL)]
    return _ffn_call(arrays, specs, st.n_tiles, ta, consts, False)


def _ffn_last(x, first_tile, n_tiles, consts):
    spec = pl.BlockSpec((FFN_TILE, D_MODEL), lambda j: (j + first_tile, 0))
    return _ffn_call([x], [spec], n_tiles, 0, consts, True)


class _Pieces:
    def __init__(self, thunks=()):
        self._thunks = list(thunks)

    def run(self, k=None):
        k = len(self._thunks) if k is None else min(k, len(self._thunks))
        for thunk in self._thunks[:k]:
            thunk()
        del self._thunks[:k]


def _two_phase(j, n_tiles, first, second_pieces, slots):
    @pl.when(j == 0)
    def _():
        first(slots[0], _Pieces())

    for parity in (0, 1):
        @pl.when(jnp.logical_and(jnp.logical_and(j > 0, j < n_tiles), j % 2 == parity))
        def _():
            pieces = second_pieces(slots[1 - parity])
            first(slots[parity], pieces)
            pieces.run()

    @pl.when(j == n_tiles)
    def _():
        second_pieces(slots[(n_tiles - 1) % 2]).run()


def _project_out_pieces(w_ref, o_ref, slot):
    x_scr, y_scr = slot
    half = x_scr.shape[0] // 2

    def piece(k, r):
        rows = slice(r * half, (r + 1) * half)
        cols = slice(k * V7X_MXU_DIM, (k + 1) * V7X_MXU_DIM)
        o_ref[rows, cols] = x_scr[rows, cols] + jnp.dot(y_scr[rows, :], w_ref[:, cols],
                                                        preferred_element_type=F32)

    return _Pieces(functools.partial(piece, k, r)
                   for k in range(D_MODEL // V7X_MXU_DIM) for r in range(2))


def _mix_kernel(x_ref, xp_ref, xn_ref, win_ref, cw_ref, ws_ref, bs_ref, lng_ref,
                lnb_ref, wout_ref, o_ref, next_scr,
                x0, x1, y0, y1, *, st):
    j = pl.program_id(0)
    _two_phase(j, st.n_tiles,
               functools.partial(_mix_first, x_ref, xp_ref, xn_ref, win_ref, cw_ref, ws_ref,
                                 bs_ref, lng_ref, lnb_ref, next_scr, st=st, j=j),
               functools.partial(_project_out_pieces, wout_ref, o_ref),
               ((x0, y0), (x1, y1)))


def _mix_first(x_ref, xp_ref, xn_ref, win_ref, cw_ref, ws_ref, bs_ref, lng_ref, lnb_ref,
               next_scr, slot, pieces, *, st, j):
    t = st.t
    x_scr, ymix_scr = slot
    _, pos, per = st.seq_pos(j)
    x = x_ref[...]
    x_scr[...] = x
    next_scr[0:HALO, :] = _rms(xp_ref[...]).astype(BF16)
    next_scr[HALO:HALO + t, :] = _rms(x).astype(BF16)
    next_scr[HALO + t:, :] = _rms(xn_ref[...]).astype(BF16)
    n = next_scr[HALO:HALO + t, :]
    pieces.run(3)

    half = t // 2
    puv = jnp.concatenate([jnp.dot(n[r * half:(r + 1) * half], win_ref[:, 3 * D_CONV:],
                                   preferred_element_type=F32) for r in range(2)],
                          axis=0)
    gu = _gelu(puv[:, :D_GMLP])
    gv = _gelu(puv[:, D_GMLP:])
    mu = jnp.mean(gv, axis=-1, keepdims=True)
    vc = gv - mu
    vln = (vc * lax.rsqrt(jnp.mean(vc * vc, axis=-1, keepdims=True) + EPS) * lng_ref[...]
           + lnb_ref[...]).astype(BF16)

    pcx = jnp.dot(next_scr[...], win_ref[:, D_CONV:3 * D_CONV], preferred_element_type=F32)
    z = pcx[:, :D_CONV] * pcx[:, D_CONV:]
    z = jnp.concatenate([jnp.where(pos == 0, 0.0, z[:HALO]), z[HALO:HALO + t],
                         jnp.where(pos == per - 1, 0.0, z[HALO + t:])], axis=0)
    rows_ext = t + 2 * HALO
    cw = cw_ref[...]
    conv = (cw[0:1] * pltpu.roll(z, 1, 0)[HALO:HALO + t] + cw[1:2] * z[HALO:HALO + t]
            + cw[2:3] * pltpu.roll(z, rows_ext - 1, 0)[HALO:HALO + t])
    pb = jnp.dot(n, win_ref[:, 0:D_CONV], preferred_element_type=F32)
    ymix_scr[:, 0:D_CONV] = _rms(pb * conv).astype(BF16)

    first_head = lax.broadcasted_iota(jnp.int32, (CHUNK, 128), 1) < (D_GMLP // N_GMLP_HEADS)
    chunks = []
    n_chunks = t // CHUNK
    for c in range(n_chunks):
        if c % 2 == 0 and c < 6:
            pieces.run(1)
        vck = vln[c * CHUNK:(c + 1) * CHUNK]
        pairs = []
        for hp in range(N_GMLP_HEADS // 2):
            r = jnp.dot(ws_ref[hp], vck[:, hp * 128:(hp + 1) * 128], preferred_element_type=F32)
            pairs.append(jnp.where(first_head, r[:CHUNK], r[CHUNK:]))
        chunks.append(jnp.concatenate(pairs, axis=1) + bs_ref[...])
    mixed = jnp.concatenate(chunks, axis=0)
    ymix_scr[:, D_CONV:] = _rms(gu * mixed).astype(BF16)


def _mix(x, st, consts):
    t = st.t
    per_tile = t // HALO
    last_halo = st.n_tok // HALO - 1
    tile = lambda j: jnp.minimum(j, st.n_tiles - 1)
    row = pl.BlockSpec((t, D_MODEL), lambda j: (tile(j), 0))
    prev = pl.BlockSpec((HALO, D_MODEL), lambda j: (jnp.maximum(tile(j) * per_tile - 1, 0), 0))
    nxt = pl.BlockSpec((HALO, D_MODEL),
                       lambda j: (jnp.minimum((tile(j) + 1) * per_tile, last_halo), 0))
    return pl.pallas_call(
        functools.partial(_mix_kernel, st=st),
        grid=(st.n_tiles + 1,),
        in_specs=[row, prev, nxt] + [_resident(a.shape) for a in consts],
        out_specs=pl.BlockSpec((t, D_MODEL), lambda j: (jnp.maximum(j - 1, 0), 0)),
        out_shape=jax.ShapeDtypeStruct(x.shape, F32),
        scratch_shapes=[pltpu.VMEM((t + 2 * HALO, D_MODEL), BF16)] + _slot_scratch(t),
        compiler_params=_params(),
        name="mix",
    )(x, x, x, *consts)


def _kv_kernel(mem_a_ref, mem_b_ref, wkv_ref, kt_ref, v_ref, *, seqs_a):
    mem = jnp.where(pl.program_id(0) < seqs_a, mem_a_ref[...], mem_b_ref[...])
    n = _rms(mem).astype(BF16)
    kv = jnp.dot(n, wkv_ref[...], preferred_element_type=F32)
    for h in range(N_XHEADS):
        kt_ref[h] = kv[:, h * XHEAD_DIM:(h + 1) * XHEAD_DIM].T.astype(BF16)
    v_ref[...] = kv[:, D_MODEL:].astype(BF16)


def _kv(mem_a, mem_b, wkv):
    ba = mem_a.shape[0]
    b = ba + mem_b.shape[0]
    return pl.pallas_call(
        functools.partial(_kv_kernel, seqs_a=ba),
        grid=(b,),
        in_specs=[pl.BlockSpec((None, N_MEM, D_MODEL), lambda bi: (jnp.minimum(bi, ba - 1), 0, 0)),
                  pl.BlockSpec((None, N_MEM, D_MODEL), lambda bi: (jnp.maximum(bi - ba, 0), 0, 0)),
                  _resident(wkv.shape)],
        out_specs=[pl.BlockSpec((None, N_XHEADS, XHEAD_DIM, N_MEM), lambda bi: (bi, 0, 0, 0)),
                   pl.BlockSpec((None, N_MEM, D_MODEL), lambda bi: (bi, 0, 0))],
        out_shape=[jax.ShapeDtypeStruct((b, N_XHEADS, XHEAD_DIM, N_MEM), BF16),
                   jax.ShapeDtypeStruct((b, N_MEM, D_MODEL), BF16)],
        compiler_params=_params(),
        name="kv_proj",
    )(mem_a, mem_b, wkv)


def _xattn_kernel(x_ref, wq_ref, kt_ref, v_ref, wo_ref, o_ref, x0, x1, y0, y1, *, n_tiles):
    _two_phase(pl.program_id(0), n_tiles,
               functools.partial(_attend, x_ref, wq_ref, kt_ref, v_ref),
               functools.partial(_project_out_pieces, wo_ref, o_ref),
               ((x0, y0), (x1, y1)))


def _attend(x_ref, wq_ref, kt_ref, v_ref, slot, pieces):
    x_scr, o_scr = slot
    x = x_ref[...]
    x_scr[...] = x
    n = _rms(x).astype(BF16)
    pieces.run(2)
    half = n.shape[0] // 2
    q = jnp.concatenate([jnp.dot(n[r * half:(r + 1) * half], wq_ref[...],
                                 preferred_element_type=F32) for r in range(2)], axis=0)
    q = (q * (XHEAD_DIM ** -0.5)).astype(BF16)

    def scores(h):
        return jnp.dot(q[:, h * XHEAD_DIM:(h + 1) * XHEAD_DIM], kt_ref[h],
                       preferred_element_type=F32)

    s_next = scores(0)
    for h in range(N_XHEADS):
        cols = slice(h * XHEAD_DIM, (h + 1) * XHEAD_DIM)
        s = s_next
        if h + 1 < N_XHEADS:
            s_next = scores(h + 1)
        pieces.run(2 if h < 2 else 1)
        p = jnp.exp(s - jnp.max(s, axis=-1, keepdims=True))
        attn = (p * (1.0 / jnp.sum(p, axis=-1, keepdims=True))).astype(BF16)
        o_scr[:, cols] = jnp.dot(attn, v_ref[:, cols], preferred_element_type=F32).astype(BF16)


def _xattn(x, st, wq, kt, v, wo):
    t = st.t
    tile = lambda j: jnp.minimum(j, st.n_tiles - 1)
    seq = lambda j: st.seq_pos(tile(j))[0]
    return pl.pallas_call(
        functools.partial(_xattn_kernel, n_tiles=st.n_tiles),
        grid=(st.n_tiles + 1,),
        in_specs=[pl.BlockSpec((t, D_MODEL), lambda j: (tile(j), 0)),
                  _resident(wq.shape),
                  pl.BlockSpec((None, N_XHEADS, XHEAD_DIM, N_MEM), lambda j: (seq(j), 0, 0, 0)),
                  pl.BlockSpec((None, N_MEM, D_MODEL), lambda j: (seq(j), 0, 0)),
                  _resident(wo.shape)],
        out_specs=pl.BlockSpec((t, D_MODEL), lambda j: (jnp.maximum(j - 1, 0), 0)),
        out_shape=jax.ShapeDtypeStruct(x.shape, F32),
        scratch_shapes=_slot_scratch(t),
        compiler_params=_params(),
        name="xattn",
    )(x, wq, kt, v, wo)


def kernel(x_prompt, x_sample, mem_prompt, mem_sample, norm_ffn1, ffn1_w_gu, ffn1_w_down, norm_mix, w_in, conv_w, gmlp_w_s, gmlp_b_s, gmlp_ln_g, gmlp_ln_b, norm_conv_out, norm_gmlp_out, w_out, norm_mem, norm_cross, xattn_w_q, xattn_w_kv, xattn_w_o, norm_ffn2, ffn2_w_gu, ffn2_w_down, norm_final):
    assert ffn1_w_gu.shape[0] == 1, "single layer"
    row = lambda a: a.reshape(1, -1).astype(F32)
    bf = lambda a: a[0].astype(BF16)
    ws = gmlp_w_s[0].reshape(N_GMLP_HEADS // 2, 2 * CHUNK, CHUNK).astype(BF16)
    bs = jnp.repeat(gmlp_b_s[0].T, D_GMLP // N_GMLP_HEADS, axis=1).astype(F32)
    g_fin = row(norm_final)
    ffn1 = (_fold_gain(norm_ffn1[0], ffn1_w_gu[0]), bf(ffn1_w_down), g_fin)
    ffn2 = (_fold_gain(norm_ffn2[0], ffn2_w_gu[0]), bf(ffn2_w_down), g_fin)
    g_branches = jnp.concatenate([norm_conv_out[0], norm_gmlp_out[0]])
    mix = (_fold_gain(norm_mix[0], w_in[0]), conv_w[0].astype(F32), ws, bs, row(gmlp_ln_g[0]),
           row(gmlp_ln_b[0]), _fold_gain(g_branches, w_out[0]))

    x1 = _ffn_first(x_prompt, x_sample, ffn1)
    x2 = _mix(x1, _Stream(x_prompt.shape, x_sample.shape, MIX_TILE), mix)
    kt, v = _kv(mem_prompt, mem_sample, _fold_gain(norm_mem[0], xattn_w_kv[0]))
    x3 = _xattn(x2, _Stream(x_prompt.shape, x_sample.shape, XATTN_TILE),
                _fold_gain(norm_cross[0], xattn_w_q[0]), kt, v, bf(xattn_w_o))
    st = _Stream(x_prompt.shape, x_sample.shape, FFN_TILE)
    y_prompt = _ffn_last(x3, 0, st.tiles_a, ffn2).reshape(x_prompt.shape)
    y_sample = _ffn_last(x3, st.tiles_a, st.tiles_b, ffn2).reshape(x_sample.shape)
    return y_prompt, y_sample
```

```python
import functools

import jax
import jax.numpy as jnp
from jax import lax
from jax.experimental import pallas as pl
from jax.experimental.pallas import tpu as pltpu

F32 = jnp.float32
BF16 = jnp.bfloat16

EPS = 1e-6
D_MODEL = 1024
D_FF = 2816
D_CONV = 512
D_GMLP = 512
N_GMLP_HEADS = 8
CHUNK = 128
N_MEM = 256
N_XHEADS = 4
XHEAD_DIM = D_MODEL // N_XHEADS

V7X_MXU_DIM = 256
V7X_BF16_SUBLANES = 16
V7X_VMEM_BYTES = 64 << 20

FF_CHUNK = V7X_MXU_DIM
N_FF_CHUNKS = D_FF // FF_CHUNK
assert N_FF_CHUNKS * FF_CHUNK == D_FF
FFN_TILE = 1024
MIX_TILE = 1024
XATTN_TILE = 1024
HALO = V7X_BF16_SUBLANES
VMEM_LIMIT = V7X_VMEM_BYTES - (8 << 20)


def _rms(x, g=None):
    ms = jnp.mean(x * x, axis=-1, keepdims=True)
    y = x * lax.rsqrt(ms + EPS)
    return y if g is None else y * g


def _gelu(x):
    c = (2.0 / jnp.pi) ** 0.5
    h = 0.5 * x
    return h + h * jnp.tanh(x * (c + (0.044715 * c) * (x * x)))


def _fold_gain(g, w):
    return (g.astype(F32)[:, None] * w.astype(F32)).astype(BF16)


def _resident(shape):
    zeros = (0,) * len(shape)
    return pl.BlockSpec(shape, lambda *_: zeros, pipeline_mode=pl.Buffered(1))


def _params():
    return pltpu.CompilerParams(vmem_limit_bytes=VMEM_LIMIT)


def _slot_scratch(t):
    return [pltpu.VMEM((t, D_MODEL), BF16), pltpu.VMEM((t, D_MODEL), BF16)]


class _Stream:
    def __init__(self, shape_a, shape_b, t):
        (ba, sa, _), (bb, sb, _) = shape_a, shape_b
        assert sa % t == 0 and sb % t == 0 and t % CHUNK == 0
        self.t = t
        self.tiles_a, self.tiles_b = ba * sa // t, bb * sb // t
        self.per_a, self.per_b = sa // t, sb // t
        self.seqs_a = ba
        self.n_tiles = self.tiles_a + self.tiles_b
        self.n_tok = self.n_tiles * t

    def seq_pos(self, j):
        in_a = j < self.tiles_a
        k = jnp.where(in_a, j, j - self.tiles_a)
        per = jnp.where(in_a, self.per_a, self.per_b)
        seq = jnp.where(in_a, k // self.per_a, self.seqs_a + k // self.per_b)
        return seq, k % per, per


def _ffn_kernel(*refs, tiles_a, final_norm):
    *x_refs, wgu_ref, wd_ref, gfin_ref, o_ref = refs
    if len(x_refs) == 2:
        x = jnp.where(pl.program_id(0) < tiles_a, x_refs[0][...], x_refs[1][...])
    else:
        x = x_refs[0][...]
    n = _rms(x).astype(BF16)
    n_parts = 2 if final_norm else 1
    part = FFN_TILE // n_parts
    accs = [None] * n_parts
    for c in range(N_FF_CHUNKS):
        for r in range(n_parts):
            nr = n[r * part:(r + 1) * part]
            gate = jnp.dot(nr, wgu_ref[:, c * FF_CHUNK:(c + 1) * FF_CHUNK],
                           preferred_element_type=F32)
            up = jnp.dot(nr, wgu_ref[:, D_FF + c * FF_CHUNK:D_FF + (c + 1) * FF_CHUNK],
                         preferred_element_type=F32)
            a = (jax.nn.silu(gate) * up).astype(BF16)
            d = jnp.dot(a, wd_ref[c * FF_CHUNK:(c + 1) * FF_CHUNK, :],
                        preferred_element_type=F32)
            accs[r] = d if accs[r] is None else accs[r] + d
    y = x + 0.5 * jnp.concatenate(accs, axis=0)
    if final_norm:
        y = _rms(y, gfin_ref[...])
    o_ref[...] = y


def _ffn_call(x_arrays, x_specs, n_tiles, tiles_a, consts, final_norm):
    t = FFN_TILE
    return pl.pallas_call(
        functools.partial(_ffn_kernel, tiles_a=tiles_a, final_norm=final_norm),
        grid=(n_tiles,),
        in_specs=list(x_specs) + [_resident(a.shape) for a in consts],
        out_specs=pl.BlockSpec((t, D_MODEL), lambda j: (j, 0)),
        out_shape=jax.ShapeDtypeStruct((n_tiles * t, D_MODEL), F32),
        compiler_params=_params(),
        name="ffn_final" if final_norm else "ffn",
    )(*x_arrays, *consts)


def _ffn_first(xa, xb, consts):
    t = FFN_TILE
    st = _Stream(xa.shape, xb.shape, t)
    ta = st.tiles_a
    specs = [pl.BlockSpec((t, D_MODEL), lambda j: (jnp.minimum(j, ta - 1), 0)),
             pl.BlockSpec((t, D_MODEL), lambda j: (jnp.maximum(j - ta, 0), 0))]
    arrays = [xa.reshape(-1, D_MODEL), xb.reshape(-1, D_MODEL)]
    return _ffn_call(arrays, specs, st.n_tiles, ta, consts, False)


def _ffn_last(x, first_tile, n_tiles, consts):
    spec = pl.BlockSpec((FFN_TILE, D_MODEL), lambda j: (j + first_tile, 0))
    return _ffn_call([x], [spec], n_tiles, 0, consts, True)


class _Pieces:
    def __init__(self, thunks=()):
        self._thunks = list(thunks)

    def run(self, k=None):
        k = len(self._thunks) if k is None else min(k, len(self._thunks))
        for thunk in self._thunks[:k]:
            thunk()
        del self._thunks[:k]


def _two_phase(j, n_tiles, first, second_pieces, slots):
    @pl.when(j == 0)
    def _():
        first(slots[0], _Pieces())

    for parity in (0, 1):
        @pl.when(jnp.logical_and(jnp.logical_and(j > 0, j < n_tiles), j % 2 == parity))
        def _():
            pieces = second_pieces(slots[1 - parity])
            first(slots[parity], pieces)
            pieces.run()

    @pl.when(j == n_tiles)
    def _():
        second_pieces(slots[(n_tiles - 1) % 2]).run()


def _project_out_pieces(xprev_ref, w_ref, o_ref, y_scr):
    half = y_scr.shape[0] // 2

    def piece(k, r):
        rows = slice(r * half, (r + 1) * half)
        cols = slice(k * V7X_MXU_DIM, (k + 1) * V7X_MXU_DIM)
        o_ref[rows, cols] = xprev_ref[rows, cols] + jnp.dot(y_scr[rows, :], w_ref[:, cols],
                                                            preferred_element_type=F32)

    return _Pieces(functools.partial(piece, k, r)
                   for k in range(D_MODEL // V7X_MXU_DIM) for r in range(2))


def _mix_kernel(x_ref, xp_ref, xn_ref, xprev_ref, win_ref, cw_ref, ws_ref, bs_ref, lng_ref,
                lnb_ref, wout_ref, o_ref, next_scr, y0, y1, *, st):
    j = pl.program_id(0)
    _two_phase(j, st.n_tiles,
               functools.partial(_mix_first, x_ref, xp_ref, xn_ref, win_ref, cw_ref, ws_ref,
                                 bs_ref, lng_ref, lnb_ref, next_scr, st=st, j=j),
               functools.partial(_project_out_pieces, xprev_ref, wout_ref, o_ref), (y0, y1))


def _mix_first(x_ref, xp_ref, xn_ref, win_ref, cw_ref, ws_ref, bs_ref, lng_ref, lnb_ref,
               next_scr, slot, pieces, *, st, j):
    t = st.t
    ymix_scr = slot
    _, pos, per = st.seq_pos(j)
    x = x_ref[...]
    next_scr[0:HALO, :] = _rms(xp_ref[...]).astype(BF16)
    next_scr[HALO:HALO + t, :] = _rms(x).astype(BF16)
    next_scr[HALO + t:, :] = _rms(xn_ref[...]).astype(BF16)
    n = next_scr[HALO:HALO + t, :]
    pieces.run(3)

    half = t // 2
    puv = jnp.concatenate([jnp.dot(n[r * half:(r + 1) * half], win_ref[:, 3 * D_CONV:],
                                   preferred_element_type=F32) for r in range(2)],
                          axis=0)
    gu = _gelu(puv[:, :D_GMLP])
    gv = _gelu(puv[:, D_GMLP:])
    mu = jnp.mean(gv, axis=-1, keepdims=True)
    vc = gv - mu
    vln = (vc * lax.rsqrt(jnp.mean(vc * vc, axis=-1, keepdims=True) + EPS) * lng_ref[...]
           + lnb_ref[...]).astype(BF16)

    pcx = jnp.dot(next_scr[...], win_ref[:, D_CONV:3 * D_CONV], preferred_element_type=F32)
    z = pcx[:, :D_CONV] * pcx[:, D_CONV:]
    z = jnp.concatenate([jnp.where(pos == 0, 0.0, z[:HALO]), z[HALO:HALO + t],
                         jnp.where(pos == per - 1, 0.0, z[HALO + t:])], axis=0)
    rows_ext = t + 2 * HALO
    cw = cw_ref[...]
    conv = (cw[0:1] * pltpu.roll(z, 1, 0)[HALO:HALO + t] + cw[1:2] * z[HALO:HALO + t]
            + cw[2:3] * pltpu.roll(z, rows_ext - 1, 0)[HALO:HALO + t])
    pb = jnp.dot(n, win_ref[:, 0:D_CONV], preferred_element_type=F32)
    ymix_scr[:, 0:D_CONV] = _rms(pb * conv).astype(BF16)

    first_head = lax.broadcasted_iota(jnp.int32, (CHUNK, 128), 1) < (D_GMLP // N_GMLP_HEADS)
    chunks = []
    n_chunks = t // CHUNK
    for c in range(n_chunks):
        if c % 2 == 0 and c < 6:
            pieces.run(1)
        vck = vln[c * CHUNK:(c + 1) * CHUNK]
        pairs = []
        for hp in range(N_GMLP_HEADS // 2):
            r = jnp.dot(ws_ref[hp], vck[:, hp * 128:(hp + 1) * 128], preferred_element_type=F32)
            pairs.append(jnp.where(first_head, r[:CHUNK], r[CHUNK:]))
        chunks.append(jnp.concatenate(pairs, axis=1) + bs_ref[...])
    mixed = jnp.concatenate(chunks, axis=0)
    ymix_scr[:, D_CONV:] = _rms(gu * mixed).astype(BF16)


def _mix(x, st, consts):
    t = st.t
    per_tile = t // HALO
    last_halo = st.n_tok // HALO - 1
    tile = lambda j: jnp.minimum(j, st.n_tiles - 1)
    row = pl.BlockSpec((t, D_MODEL), lambda j: (tile(j), 0))
    row_prev = pl.BlockSpec((t, D_MODEL), lambda j: (jnp.maximum(j - 1, 0), 0))
    prev = pl.BlockSpec((HALO, D_MODEL), lambda j: (jnp.maximum(tile(j) * per_tile - 1, 0), 0))
    nxt = pl.BlockSpec((HALO, D_MODEL),
                       lambda j: (jnp.minimum((tile(j) + 1) * per_tile, last_halo), 0))
    return pl.pallas_call(
        functools.partial(_mix_kernel, st=st),
        grid=(st.n_tiles + 1,),
        in_specs=[row, prev, nxt, row_prev] + [_resident(a.shape) for a in consts],
        out_specs=row_prev,
        out_shape=jax.ShapeDtypeStruct(x.shape, F32),
        scratch_shapes=[pltpu.VMEM((t + 2 * HALO, D_MODEL), BF16)] + _slot_scratch(t),
        compiler_params=_params(),
        name="mix",
    )(x, x, x, x, *consts)


def _kv_kernel(mem_a_ref, mem_b_ref, wkv_ref, kt_ref, v_ref, *, seqs_a):
    mem = jnp.where(pl.program_id(0) < seqs_a, mem_a_ref[...], mem_b_ref[...])
    n = _rms(mem).astype(BF16)
    kv = jnp.dot(n, wkv_ref[...], preferred_element_type=F32)
    for h in range(N_XHEADS):
        kt_ref[h] = kv[:, h * XHEAD_DIM:(h + 1) * XHEAD_DIM].T.astype(BF16)
    v_ref[...] = kv[:, D_MODEL:].astype(BF16)


def _kv(mem_a, mem_b, wkv):
    ba = mem_a.shape[0]
    b = ba + mem_b.shape[0]
    return pl.pallas_call(
        functools.partial(_kv_kernel, seqs_a=ba),
        grid=(b,),
        in_specs=[pl.BlockSpec((None, N_MEM, D_MODEL), lambda bi: (jnp.minimum(bi, ba - 1), 0, 0)),
                  pl.BlockSpec((None, N_MEM, D_MODEL), lambda bi: (jnp.maximum(bi - ba, 0), 0, 0)),
                  _resident(wkv.shape)],
        out_specs=[pl.BlockSpec((None, N_XHEADS, XHEAD_DIM, N_MEM), lambda bi: (bi, 0, 0, 0)),
                   pl.BlockSpec((None, N_MEM, D_MODEL), lambda bi: (bi, 0, 0))],
        out_shape=[jax.ShapeDtypeStruct((b, N_XHEADS, XHEAD_DIM, N_MEM), BF16),
                   jax.ShapeDtypeStruct((b, N_MEM, D_MODEL), BF16)],
        compiler_params=_params(),
        name="kv_proj",
    )(mem_a, mem_b, wkv)


def _xattn_kernel(x_ref, xprev_ref, wq_ref, kt_ref, v_ref, wo_ref, o_ref, y0, y1, *, n_tiles):
    _two_phase(pl.program_id(0), n_tiles,
               functools.partial(_attend, x_ref, wq_ref, kt_ref, v_ref),
               functools.partial(_project_out_pieces, xprev_ref, wo_ref, o_ref), (y0, y1))


def _attend(x_ref, wq_ref, kt_ref, v_ref, slot, pieces):
    o_scr = slot
    x = x_ref[...]
    n = _rms(x).astype(BF16)
    pieces.run(2)
    half = n.shape[0] // 2
    q = jnp.concatenate([jnp.dot(n[r * half:(r + 1) * half], wq_ref[...],
                                 preferred_element_type=F32) for r in range(2)], axis=0)
    q = (q * (XHEAD_DIM ** -0.5)).astype(BF16)

    def scores(h):
        return jnp.dot(q[:, h * XHEAD_DIM:(h + 1) * XHEAD_DIM], kt_ref[h],
                       preferred_element_type=F32)

    s_next = scores(0)
    for h in range(N_XHEADS):
        cols = slice(h * XHEAD_DIM, (h + 1) * XHEAD_DIM)
        s = s_next
        if h + 1 < N_XHEADS:
            s_next = scores(h + 1)
        pieces.run(2 if h < 2 else 1)
        p = jnp.exp(s - jnp.max(s, axis=-1, keepdims=True))
        attn = (p * (1.0 / jnp.sum(p, axis=-1, keepdims=True))).astype(BF16)
        o_scr[:, cols] = jnp.dot(attn, v_ref[:, cols], preferred_element_type=F32).astype(BF16)


def _xattn(x, st, wq, kt, v, wo):
    t = st.t
    tile = lambda j: jnp.minimum(j, st.n_tiles - 1)
    seq = lambda j: st.seq_pos(tile(j))[0]
    row_prev = pl.BlockSpec((t, D_MODEL), lambda j: (jnp.maximum(j - 1, 0), 0))
    return pl.pallas_call(
        functools.partial(_xattn_kernel, n_tiles=st.n_tiles),
        grid=(st.n_tiles + 1,),
        in_specs=[pl.BlockSpec((t, D_MODEL), lambda j: (tile(j), 0)), row_prev,
                  _resident(wq.shape),
                  pl.BlockSpec((None, N_XHEADS, XHEAD_DIM, N_MEM), lambda j: (seq(j), 0, 0, 0)),
                  pl.BlockSpec((None, N_MEM, D_MODEL), lambda j: (seq(j), 0, 0)),
                  _resident(wo.shape)],
        out_specs=row_prev,
        out_shape=jax.ShapeDtypeStruct(x.shape, F32),
        scratch_shapes=_slot_scratch(t),
        compiler_params=_params(),
        name="xattn",
    )(x, x, wq, kt, v, wo)


def kernel(x_prompt, x_sample, mem_prompt, mem_sample, norm_ffn1, ffn1_w_gu, ffn1_w_down, norm_mix, w_in, conv_w, gmlp_w_s, gmlp_b_s, gmlp_ln_g, gmlp_ln_b, norm_conv_out, norm_gmlp_out, w_out, norm_mem, norm_cross, xattn_w_q, xattn_w_kv, xattn_w_o, norm_ffn2, ffn2_w_gu, ffn2_w_down, norm_final):
    assert ffn1_w_gu.shape[0] == 1, "single layer"
    row = lambda a: a.reshape(1, -1).astype(F32)
    bf = lambda a: a[0].astype(BF16)
    ws = gmlp_w_s[0].reshape(N_GMLP_HEADS // 2, 2 * CHUNK, CHUNK).astype(BF16)
    bs = jnp.repeat(gmlp_b_s[0].T, D_GMLP // N_GMLP_HEADS, axis=1).astype(F32)
    g_fin = row(norm_final)
    ffn1 = (_fold_gain(norm_ffn1[0], ffn1_w_gu[0]), bf(ffn1_w_down), g_fin)
    ffn2 = (_fold_gain(norm_ffn2[0], ffn2_w_gu[0]), bf(ffn2_w_down), g_fin)
    g_branches = jnp.concatenate([norm_conv_out[0], norm_gmlp_out[0]])
    mix = (_fold_gain(norm_mix[0], w_in[0]), conv_w[0].astype(F32), ws, bs, row(gmlp_ln_g[0]),
           row(gmlp_ln_b[0]), _fold_gain(g_branches, w_out[0]))

    x1 = _ffn_first(x_prompt, x_sample, ffn1)
    x2 = _mix(x1, _Stream(x_prompt.shape, x_sample.shape, MIX_TILE), mix)
    kt, v = _kv(mem_prompt, mem_sample, _fold_gain(norm_mem[0], xattn_w_kv[0]))
    x3 = _xattn(x2, _Stream(x_prompt.shape, x_sample.shape, XATTN_TILE),
                _fold_gain(norm_cross[0], xattn_w_q[0]), kt, v, bf(xattn_w_o))
    st = _Stream(x_prompt.shape, x_sample.shape, FFN_TILE)
    y_prompt = _ffn_last(x3, 0, st.tiles_a, ffn2).reshape(x_prompt.shape)
    y_sample = _ffn_last(x3, st.tiles_a, st.tiles_b, ffn2).reshape(x_sample.shape)
    return y_prompt, y_sample
```

```python
import functools

import jax
import jax.numpy as jnp
from jax import lax
from jax.experimental import pallas as pl
from jax.experimental.pallas import tpu as pltpu

F32 = jnp.float32
BF16 = jnp.bfloat16

EPS = 1e-6
D_MODEL = 1024
D_FF = 2816
D_CONV = 512
D_GMLP = 512
N_GMLP_HEADS = 8
CHUNK = 128
N_MEM = 256
N_XHEADS = 4
XHEAD_DIM = D_MODEL // N_XHEADS

V7X_MXU_DIM = 256
V7X_BF16_SUBLANES = 16
V7X_VMEM_BYTES = 64 << 20

FF_CHUNK = V7X_MXU_DIM
FF_CHUNKS = [(lo, min(lo + FF_CHUNK, D_FF)) for lo in range(0, D_FF, FF_CHUNK)]
assert D_FF % V7X_MXU_DIM == 0
FFN_TILE = 1024
MIX_TILE = 1024
XATTN_TILE = 1024
HALO = V7X_BF16_SUBLANES
VMEM_LIMIT = V7X_VMEM_BYTES - (8 << 20)


def _rms(x, g=None):
    ms = jnp.mean(x * x, axis=-1, keepdims=True)
    y = x * lax.rsqrt(ms + EPS)
    return y if g is None else y * g


def _gelu(x):
    c = (2.0 / jnp.pi) ** 0.5
    h = 0.5 * x
    return h + h * jnp.tanh(x * (c + (0.044715 * c) * (x * x)))


def _fold_gain(g, w):
    return (g.astype(F32)[:, None] * w.astype(F32)).astype(BF16)


def _resident(shape):
    zeros = (0,) * len(shape)
    return pl.BlockSpec(shape, lambda *_: zeros, pipeline_mode=pl.Buffered(1))


def _params():
    return pltpu.CompilerParams(vmem_limit_bytes=VMEM_LIMIT)


def _slot_scratch(t):
    return [pltpu.VMEM((t, D_MODEL), BF16), pltpu.VMEM((t, D_MODEL), BF16)]


class _Stream:
    def __init__(self, shape_a, shape_b, t):
        (ba, sa, _), (bb, sb, _) = shape_a, shape_b
        assert sa % t == 0 and sb % t == 0 and t % CHUNK == 0
        self.t = t
        self.tiles_a, self.tiles_b = ba * sa // t, bb * sb // t
        self.per_a, self.per_b = sa // t, sb // t
        self.seqs_a = ba
        self.n_tiles = self.tiles_a + self.tiles_b
        self.n_tok = self.n_tiles * t

    def seq_pos(self, j):
        in_a = j < self.tiles_a
        k = jnp.where(in_a, j, j - self.tiles_a)
        per = jnp.where(in_a, self.per_a, self.per_b)
        seq = jnp.where(in_a, k // self.per_a, self.seqs_a + k // self.per_b)
        return seq, k % per, per


def _ffn_kernel(*refs, tiles_a, final_norm):
    *x_refs, wgu_ref, wd_ref, gfin_ref, o_ref, a_scr = refs
    if len(x_refs) == 2:
        x = jnp.where(pl.program_id(0) < tiles_a, x_refs[0][...], x_refs[1][...])
    else:
        x = x_refs[0][...]
    n = _rms(x).astype(BF16)
    n_parts = 2 if final_norm else 1
    part = FFN_TILE // n_parts
    for lo, hi in FF_CHUNKS:
        for r in range(n_parts):
            rows = slice(r * part, (r + 1) * part)
            gate = jnp.dot(n[rows], wgu_ref[:, lo:hi], preferred_element_type=F32)
            up = jnp.dot(n[rows], wgu_ref[:, D_FF + lo:D_FF + hi], preferred_element_type=F32)
            a_scr[rows, lo:hi] = (jax.nn.silu(gate) * up).astype(BF16)
    for r in range(n_parts):
        rows = slice(r * part, (r + 1) * part)
        y = x[rows] + 0.5 * jnp.dot(a_scr[rows, :], wd_ref[...], preferred_element_type=F32)
        if final_norm:
            y = _rms(y, gfin_ref[...])
        o_ref[rows, :] = y


def _ffn_call(x_arrays, x_specs, n_tiles, tiles_a, consts, final_norm):
    t = FFN_TILE
    return pl.pallas_call(
        functools.partial(_ffn_kernel, tiles_a=tiles_a, final_norm=final_norm),
        grid=(n_tiles,),
        in_specs=list(x_specs) + [_resident(a.shape) for a in consts],
        out_specs=pl.BlockSpec((t, D_MODEL), lambda j: (j, 0)),
        out_shape=jax.ShapeDtypeStruct((n_tiles * t, D_MODEL), F32),
        scratch_shapes=[pltpu.VMEM((t, D_FF), BF16)],
        compiler_params=_params(),
        name="ffn_final" if final_norm else "ffn",
    )(*x_arrays, *consts)


def _ffn_first(xa, xb, consts):
    t = FFN_TILE
    st = _Stream(xa.shape, xb.shape, t)
    ta = st.tiles_a
    specs = [pl.BlockSpec((t, D_MODEL), lambda j: (jnp.minimum(j, ta - 1), 0)),
             pl.BlockSpec((t, D_MODEL), lambda j: (jnp.maximum(j - ta, 0), 0))]
    arrays = [xa.reshape(-1, D_MODEL), xb.reshape(-1, D_MODEL)]
    return _ffn_call(arrays, specs, st.n_tiles, ta, consts, False)


def _ffn_last(x, first_tile, n_tiles, consts):
    spec = pl.BlockSpec((FFN_TILE, D_MODEL), lambda j: (j + first_tile, 0))
    return _ffn_call([x], [spec], n_tiles, 0, consts, True)


class _Pieces:
    def __init__(self, thunks=()):
        self._thunks = list(thunks)

    def run(self, k=None):
        k = len(self._thunks) if k is None else min(k, len(self._thunks))
        for thunk in self._thunks[:k]:
            thunk()
        del self._thunks[:k]


def _two_phase(j, n_tiles, first, second_pieces, slots):
    @pl.when(j == 0)
    def _():
        first(slots[0], _Pieces())

    for parity in (0, 1):
        @pl.when(jnp.logical_and(jnp.logical_and(j > 0, j < n_tiles), j % 2 == parity))
        def _():
            pieces = second_pieces(slots[1 - parity])
            first(slots[parity], pieces)
            pieces.run()

    @pl.when(j == n_tiles)
    def _():
        second_pieces(slots[(n_tiles - 1) % 2]).run()


def _project_out_pieces(xprev_ref, w_ref, o_ref, y_scr):
    half = y_scr.shape[0] // 2

    def piece(k, r):
        rows = slice(r * half, (r + 1) * half)
        cols = slice(k * V7X_MXU_DIM, (k + 1) * V7X_MXU_DIM)
        o_ref[rows, cols] = xprev_ref[rows, cols] + jnp.dot(y_scr[rows, :], w_ref[:, cols],
                                                            preferred_element_type=F32)

    return _Pieces(functools.partial(piece, k, r)
                   for k in range(D_MODEL // V7X_MXU_DIM) for r in range(2))


def _mix_kernel(x_ref, xp_ref, xn_ref, xprev_ref, win_ref, cw_ref, ws_ref, bs_ref, lng_ref,
                lnb_ref, wout_ref, o_ref, next_scr, y0, y1, *, st):
    j = pl.program_id(0)
    _two_phase(j, st.n_tiles,
               functools.partial(_mix_first, x_ref, xp_ref, xn_ref, win_ref, cw_ref, ws_ref,
                                 bs_ref, lng_ref, lnb_ref, next_scr, st=st, j=j),
               functools.partial(_project_out_pieces, xprev_ref, wout_ref, o_ref), (y0, y1))


def _mix_first(x_ref, xp_ref, xn_ref, win_ref, cw_ref, ws_ref, bs_ref, lng_ref, lnb_ref,
               next_scr, slot, pieces, *, st, j):
    t = st.t
    ymix_scr = slot
    _, pos, per = st.seq_pos(j)
    x = x_ref[...]
    next_scr[0:HALO, :] = _rms(xp_ref[...]).astype(BF16)
    next_scr[HALO:HALO + t, :] = _rms(x).astype(BF16)
    next_scr[HALO + t:, :] = _rms(xn_ref[...]).astype(BF16)
    n = next_scr[HALO:HALO + t, :]
    pieces.run(3)

    half = t // 2
    puv = jnp.concatenate([jnp.dot(n[r * half:(r + 1) * half], win_ref[:, 3 * D_CONV:],
                                   preferred_element_type=F32) for r in range(2)],
                          axis=0)
    gu = _gelu(puv[:, :D_GMLP])
    gv = _gelu(puv[:, D_GMLP:])
    mu = jnp.mean(gv, axis=-1, keepdims=True)
    vc = gv - mu
    vln = (vc * lax.rsqrt(jnp.mean(vc * vc, axis=-1, keepdims=True) + EPS) * lng_ref[...]
           + lnb_ref[...]).astype(BF16)

    pcx = jnp.dot(next_scr[...], win_ref[:, D_CONV:3 * D_CONV], preferred_element_type=F32)
    z = pcx[:, :D_CONV] * pcx[:, D_CONV:]
    z = jnp.concatenate([jnp.where(pos == 0, 0.0, z[:HALO]), z[HALO:HALO + t],
                         jnp.where(pos == per - 1, 0.0, z[HALO + t:])], axis=0)
    rows_ext = t + 2 * HALO
    cw = cw_ref[...]
    conv = (cw[0:1] * pltpu.roll(z, 1, 0)[HALO:HALO + t] + cw[1:2] * z[HALO:HALO + t]
            + cw[2:3] * pltpu.roll(z, rows_ext - 1, 0)[HALO:HALO + t])
    pb = jnp.dot(n, win_ref[:, 0:D_CONV], preferred_element_type=F32)
    ymix_scr[:, 0:D_CONV] = _rms(pb * conv).astype(BF16)

    first_head = lax.broadcasted_iota(jnp.int32, (CHUNK, 128), 1) < (D_GMLP // N_GMLP_HEADS)
    chunks = []
    n_chunks = t // CHUNK
    for c in range(n_chunks):
        if c % 2 == 0 and c < 6:
            pieces.run(1)
        vck = vln[c * CHUNK:(c + 1) * CHUNK]
        pairs = []
        for hp in range(N_GMLP_HEADS // 2):
            r = jnp.dot(ws_ref[hp], vck[:, hp * 128:(hp + 1) * 128], preferred_element_type=F32)
            pairs.append(jnp.where(first_head, r[:CHUNK], r[CHUNK:]))
        chunks.append(jnp.concatenate(pairs, axis=1) + bs_ref[...])
    mixed = jnp.concatenate(chunks, axis=0)
    ymix_scr[:, D_CONV:] = _rms(gu * mixed).astype(BF16)


def _mix(x, st, consts):
    t = st.t
    per_tile = t // HALO
    last_halo = st.n_tok // HALO - 1
    tile = lambda j: jnp.minimum(j, st.n_tiles - 1)
    row = pl.BlockSpec((t, D_MODEL), lambda j: (tile(j), 0))
    row_prev = pl.BlockSpec((t, D_MODEL), lambda j: (jnp.maximum(j - 1, 0), 0))
    prev = pl.BlockSpec((HALO, D_MODEL), lambda j: (jnp.maximum(tile(j) * per_tile - 1, 0), 0))
    nxt = pl.BlockSpec((HALO, D_MODEL),
                       lambda j: (jnp.minimum((tile(j) + 1) * per_tile, last_halo), 0))
    return pl.pallas_call(
        functools.partial(_mix_kernel, st=st),
        grid=(st.n_tiles + 1,),
        in_specs=[row, prev, nxt, row_prev] + [_resident(a.shape) for a in consts],
        out_specs=row_prev,
        out_shape=jax.ShapeDtypeStruct(x.shape, F32),
        scratch_shapes=[pltpu.VMEM((t + 2 * HALO, D_MODEL), BF16)] + _slot_scratch(t),
        compiler_params=_params(),
        name="mix",
    )(x, x, x, x, *consts)


def _kv_kernel(mem_a_ref, mem_b_ref, wkv_ref, kt_ref, v_ref, *, seqs_a):
    mem = jnp.where(pl.program_id(0) < seqs_a, mem_a_ref[...], mem_b_ref[...])
    n = _rms(mem).astype(BF16)
    kv = jnp.dot(n, wkv_ref[...], preferred_element_type=F32)
    for h in range(N_XHEADS):
        kt_ref[h] = kv[:, h * XHEAD_DIM:(h + 1) * XHEAD_DIM].T.astype(BF16)
    v_ref[...] = kv[:, D_MODEL:].astype(BF16)


def _kv(mem_a, mem_b, wkv):
    ba = mem_a.shape[0]
    b = ba + mem_b.shape[0]
    return pl.pallas_call(
        functools.partial(_kv_kernel, seqs_a=ba),
        grid=(b,),
        in_specs=[pl.BlockSpec((None, N_MEM, D_MODEL), lambda bi: (jnp.minimum(bi, ba - 1), 0, 0)),
                  pl.BlockSpec((None, N_MEM, D_MODEL), lambda bi: (jnp.maximum(bi - ba, 0), 0, 0)),
                  _resident(wkv.shape)],
        out_specs=[pl.BlockSpec((None, N_XHEADS, XHEAD_DIM, N_MEM), lambda bi: (bi, 0, 0, 0)),
                   pl.BlockSpec((None, N_MEM, D_MODEL), lambda bi: (bi, 0, 0))],
        out_shape=[jax.ShapeDtypeStruct((b, N_XHEADS, XHEAD_DIM, N_MEM), BF16),
                   jax.ShapeDtypeStruct((b, N_MEM, D_MODEL), BF16)],
        compiler_params=_params(),
        name="kv_proj",
    )(mem_a, mem_b, wkv)


def _xattn_kernel(x_ref, xprev_ref, wq_ref, kt_ref, v_ref, wo_ref, o_ref, y0, y1, *, n_tiles):
    _two_phase(pl.program_id(0), n_tiles,
               functools.partial(_attend, x_ref, wq_ref, kt_ref, v_ref),
               functools.partial(_project_out_pieces, xprev_ref, wo_ref, o_ref), (y0, y1))


def _attend(x_ref, wq_ref, kt_ref, v_ref, slot, pieces):
    o_scr = slot
    x = x_ref[...]
    n = _rms(x).astype(BF16)
    pieces.run(2)
    half = n.shape[0] // 2
    q = jnp.concatenate([jnp.dot(n[r * half:(r + 1) * half], wq_ref[...],
                                 preferred_element_type=F32) for r in range(2)], axis=0)
    q = (q * (XHEAD_DIM ** -0.5)).astype(BF16)

    def scores(h):
        return jnp.dot(q[:, h * XHEAD_DIM:(h + 1) * XHEAD_DIM], kt_ref[h],
                       preferred_element_type=F32)

    s_next = scores(0)
    for h in range(N_XHEADS):
        cols = slice(h * XHEAD_DIM, (h + 1) * XHEAD_DIM)
        s = s_next
        if h + 1 < N_XHEADS:
            s_next = scores(h + 1)
        pieces.run(2 if h < 2 else 1)
        p = jnp.exp(s - jnp.max(s, axis=-1, keepdims=True))
        attn = (p * (1.0 / jnp.sum(p, axis=-1, keepdims=True))).astype(BF16)
        o_scr[:, cols] = jnp.dot(attn, v_ref[:, cols], preferred_element_type=F32).astype(BF16)


def _xattn(x, st, wq, kt, v, wo):
    t = st.t
    tile = lambda j: jnp.minimum(j, st.n_tiles - 1)
    seq = lambda j: st.seq_pos(tile(j))[0]
    row_prev = pl.BlockSpec((t, D_MODEL), lambda j: (jnp.maximum(j - 1, 0), 0))
    return pl.pallas_call(
        functools.partial(_xattn_kernel, n_tiles=st.n_tiles),
        grid=(st.n_tiles + 1,),
        in_specs=[pl.BlockSpec((t, D_MODEL), lambda j: (tile(j), 0)), row_prev,
                  _resident(wq.shape),
                  pl.BlockSpec((None, N_XHEADS, XHEAD_DIM, N_MEM), lambda j: (seq(j), 0, 0, 0)),
                  pl.BlockSpec((None, N_MEM, D_MODEL), lambda j: (seq(j), 0, 0)),
                  _resident(wo.shape)],
        out_specs=row_prev,
        out_shape=jax.ShapeDtypeStruct(x.shape, F32),
        scratch_shapes=_slot_scratch(t),
        compiler_params=_params(),
        name="xattn",
    )(x, x, wq, kt, v, wo)


def kernel(x_prompt, x_sample, mem_prompt, mem_sample, norm_ffn1, ffn1_w_gu, ffn1_w_down, norm_mix, w_in, conv_w, gmlp_w_s, gmlp_b_s, gmlp_ln_g, gmlp_ln_b, norm_conv_out, norm_gmlp_out, w_out, norm_mem, norm_cross, xattn_w_q, xattn_w_kv, xattn_w_o, norm_ffn2, ffn2_w_gu, ffn2_w_down, norm_final):
    assert ffn1_w_gu.shape[0] == 1, "single layer"
    row = lambda a: a.reshape(1, -1).astype(F32)
    bf = lambda a: a[0].astype(BF16)
    ws = gmlp_w_s[0].reshape(N_GMLP_HEADS // 2, 2 * CHUNK, CHUNK).astype(BF16)
    bs = jnp.repeat(gmlp_b_s[0].T, D_GMLP // N_GMLP_HEADS, axis=1).astype(F32)
    g_fin = row(norm_final)
    ffn1 = (_fold_gain(norm_ffn1[0], ffn1_w_gu[0]), bf(ffn1_w_down), g_fin)
    ffn2 = (_fold_gain(norm_ffn2[0], ffn2_w_gu[0]), bf(ffn2_w_down), g_fin)
    g_branches = jnp.concatenate([norm_conv_out[0], norm_gmlp_out[0]])
    mix = (_fold_gain(norm_mix[0], w_in[0]), conv_w[0].astype(F32), ws, bs, row(gmlp_ln_g[0]),
           row(gmlp_ln_b[0]), _fold_gain(g_branches, w_out[0]))

    x1 = _ffn_first(x_prompt, x_sample, ffn1)
    x2 = _mix(x1, _Stream(x_prompt.shape, x_sample.shape, MIX_TILE), mix)
    kt, v = _kv(mem_prompt, mem_sample, _fold_gain(norm_mem[0], xattn_w_kv[0]))
    x3 = _xattn(x2, _Stream(x_prompt.shape, x_sample.shape, XATTN_TILE),
                _fold_gain(norm_cross[0], xattn_w_q[0]), kt, v, bf(xattn_w_o))
    st = _Stream(x_prompt.shape, x_sample.shape, FFN_TILE)
    y_prompt = _ffn_last(x3, 0, st.tiles_a, ffn2).reshape(x_prompt.shape)
    y_sample = _ffn_last(x3, st.tiles_a, st.tiles_b, ffn2).reshape(x_sample.shape)
    return y_prompt, y_sample
```

```python
import functools

import jax
import jax.numpy as jnp
from jax import lax
from jax.experimental import pallas as pl
from jax.experimental.pallas import tpu as pltpu

F32 = jnp.float32
BF16 = jnp.bfloat16

EPS = 1e-6
D_MODEL = 1024
D_FF = 2816
D_CONV = 512
D_GMLP = 512
N_GMLP_HEADS = 8
CHUNK = 128
N_MEM = 256
N_XHEADS = 4
XHEAD_DIM = D_MODEL // N_XHEADS

V7X_MXU_DIM = 256
V7X_BF16_SUBLANES = 16
V7X_VMEM_BYTES = 64 << 20

FF_CHUNK = V7X_MXU_DIM
FF_CHUNKS = [(lo, min(lo + FF_CHUNK, D_FF)) for lo in range(0, D_FF, FF_CHUNK)]
assert D_FF % V7X_MXU_DIM == 0
FFN_TILE = 1024
MIX_TILE = 1024
XATTN_TILE = 1024
PIECE_ROW_SPLITS = 1
HALO = V7X_BF16_SUBLANES
VMEM_LIMIT = V7X_VMEM_BYTES - (8 << 20)


def _rms(x, g=None):
    ms = jnp.mean(x * x, axis=-1, keepdims=True)
    y = x * lax.rsqrt(ms + EPS)
    return y if g is None else y * g


def _gelu(x):
    c = (2.0 / jnp.pi) ** 0.5
    h = 0.5 * x
    return h + h * jnp.tanh(x * (c + (0.044715 * c) * (x * x)))


def _silu(x):
    h = 0.5 * x
    return h + h * jnp.tanh(h)


def _fold_gain(g, w):
    return (g.astype(F32)[:, None] * w.astype(F32)).astype(BF16)


def _resident(shape):
    zeros = (0,) * len(shape)
    return pl.BlockSpec(shape, lambda *_: zeros, pipeline_mode=pl.Buffered(1))


def _params():
    return pltpu.CompilerParams(vmem_limit_bytes=VMEM_LIMIT)


def _slot_scratch(t):
    return [pltpu.VMEM((t, D_MODEL), BF16), pltpu.VMEM((t, D_MODEL), BF16)]


class _Stream:
    def __init__(self, shape_a, shape_b, t):
        (ba, sa, _), (bb, sb, _) = shape_a, shape_b
        assert sa % t == 0 and sb % t == 0 and t % CHUNK == 0
        self.t = t
        self.tiles_a, self.tiles_b = ba * sa // t, bb * sb // t
        self.per_a, self.per_b = sa // t, sb // t
        self.seqs_a = ba
        self.n_tiles = self.tiles_a + self.tiles_b
        self.n_tok = self.n_tiles * t

    def seq_pos(self, j):
        in_a = j < self.tiles_a
        k = jnp.where(in_a, j, j - self.tiles_a)
        per = jnp.where(in_a, self.per_a, self.per_b)
        seq = jnp.where(in_a, k // self.per_a, self.seqs_a + k // self.per_b)
        return seq, k % per, per


def _ffn_kernel(*refs, tiles_a, final_norm):
    *x_refs, wgu_ref, wd_ref, gfin_ref, o_ref, a_scr = refs
    if len(x_refs) == 2:
        x = jnp.where(pl.program_id(0) < tiles_a, x_refs[0][...], x_refs[1][...])
    else:
        x = x_refs[0][...]
    n = _rms(x).astype(BF16)
    n_parts = 2 if final_norm else 1
    part = FFN_TILE // n_parts
    for lo, hi in FF_CHUNKS:
        for r in range(n_parts):
            rows = slice(r * part, (r + 1) * part)
            gate = jnp.dot(n[rows], wgu_ref[:, lo:hi], preferred_element_type=F32)
            up = jnp.dot(n[rows], wgu_ref[:, D_FF + lo:D_FF + hi], preferred_element_type=F32)
            a_scr[rows, lo:hi] = (_silu(gate) * up).astype(BF16)
    for r in range(n_parts):
        rows = slice(r * part, (r + 1) * part)
        y = x[rows] + 0.5 * jnp.dot(a_scr[rows, :], wd_ref[...], preferred_element_type=F32)
        if final_norm:
            y = _rms(y, gfin_ref[...])
        o_ref[rows, :] = y


def _ffn_call(x_arrays, x_specs, n_tiles, tiles_a, consts, final_norm):
    t = FFN_TILE
    return pl.pallas_call(
        functools.partial(_ffn_kernel, tiles_a=tiles_a, final_norm=final_norm),
        grid=(n_tiles,),
        in_specs=list(x_specs) + [_resident(a.shape) for a in consts],
        out_specs=pl.BlockSpec((t, D_MODEL), lambda j: (j, 0)),
        out_shape=jax.ShapeDtypeStruct((n_tiles * t, D_MODEL), F32),
        scratch_shapes=[pltpu.VMEM((t, D_FF), BF16)],
        compiler_params=_params(),
        name="ffn_final" if final_norm else "ffn",
    )(*x_arrays, *consts)


def _ffn_first(xa, xb, consts):
    t = FFN_TILE
    st = _Stream(xa.shape, xb.shape, t)
    ta = st.tiles_a
    specs = [pl.BlockSpec((t, D_MODEL), lambda j: (jnp.minimum(j, ta - 1), 0)),
             pl.BlockSpec((t, D_MODEL), lambda j: (jnp.maximum(j - ta, 0), 0))]
    arrays = [xa.reshape(-1, D_MODEL), xb.reshape(-1, D_MODEL)]
    return _ffn_call(arrays, specs, st.n_tiles, ta, consts, False)


def _ffn_last(x, first_tile, n_tiles, consts):
    spec = pl.BlockSpec((FFN_TILE, D_MODEL), lambda j: (j + first_tile, 0))
    return _ffn_call([x], [spec], n_tiles, 0, consts, True)


class _Pieces:
    def __init__(self, thunks=()):
        self._thunks = list(thunks)

    def run(self, k=None):
        k = len(self._thunks) if k is None else min(k, len(self._thunks))
        for thunk in self._thunks[:k]:
            thunk()
        del self._thunks[:k]


def _two_phase(j, n_tiles, first, second_pieces, slots):
    @pl.when(j == 0)
    def _():
        first(slots[0], _Pieces())

    for parity in (0, 1):
        @pl.when(jnp.logical_and(jnp.logical_and(j > 0, j < n_tiles), j % 2 == parity))
        def _():
            pieces = second_pieces(slots[1 - parity])
            first(slots[parity], pieces)
            pieces.run()

    @pl.when(j == n_tiles)
    def _():
        second_pieces(slots[(n_tiles - 1) % 2]).run()


def _project_out_pieces(xprev_ref, w_ref, o_ref, y_scr):
    half = y_scr.shape[0] // PIECE_ROW_SPLITS

    def piece(k, r):
        rows = slice(r * half, (r + 1) * half)
        cols = slice(k * V7X_MXU_DIM, (k + 1) * V7X_MXU_DIM)
        o_ref[rows, cols] = xprev_ref[rows, cols] + jnp.dot(y_scr[rows, :], w_ref[:, cols],
                                                            preferred_element_type=F32)

    return _Pieces(functools.partial(piece, k, r)
                   for k in range(D_MODEL // V7X_MXU_DIM) for r in range(PIECE_ROW_SPLITS))


def _mix_kernel(x_ref, xn_ref, xprev_ref, win_ref, cw_ref, ws_ref, bs_ref, lng_ref,
                lnb_ref, wout_ref, o_ref, next_scr, y0, y1, *, st):
    j = pl.program_id(0)
    _two_phase(j, st.n_tiles,
               functools.partial(_mix_first, x_ref, xprev_ref, xn_ref, win_ref, cw_ref, ws_ref,
                                 bs_ref, lng_ref, lnb_ref, next_scr, st=st, j=j),
               functools.partial(_project_out_pieces, xprev_ref, wout_ref, o_ref), (y0, y1))


def _mix_first(x_ref, xprev_ref, xn_ref, win_ref, cw_ref, ws_ref, bs_ref, lng_ref, lnb_ref,
               next_scr, slot, pieces, *, st, j):
    t = st.t
    ymix_scr = slot
    _, pos, per = st.seq_pos(j)
    x = x_ref[...]
    next_scr[0:HALO, :] = _rms(xprev_ref[t - HALO:t, :]).astype(BF16)
    next_scr[HALO:HALO + t, :] = _rms(x).astype(BF16)
    next_scr[HALO + t:, :] = _rms(xn_ref[...]).astype(BF16)
    n = next_scr[HALO:HALO + t, :]
    pieces.run(1)

    half = t // 2
    puv = jnp.concatenate([jnp.dot(n[r * half:(r + 1) * half], win_ref[:, 3 * D_CONV:],
                                   preferred_element_type=F32) for r in range(2)],
                          axis=0)
    gu = _gelu(puv[:, :D_GMLP])
    gv = _gelu(puv[:, D_GMLP:])
    mu = jnp.mean(gv, axis=-1, keepdims=True)
    vc = gv - mu
    vln = (vc * lax.rsqrt(jnp.mean(vc * vc, axis=-1, keepdims=True) + EPS) * lng_ref[...]
           + lnb_ref[...]).astype(BF16)

    pcx = jnp.dot(next_scr[...], win_ref[:, D_CONV:3 * D_CONV], preferred_element_type=F32)
    z = pcx[:, :D_CONV] * pcx[:, D_CONV:]
    z = jnp.concatenate([jnp.where(pos == 0, 0.0, z[:HALO]), z[HALO:HALO + t],
                         jnp.where(pos == per - 1, 0.0, z[HALO + t:])], axis=0)
    rows_ext = t + 2 * HALO
    cw = cw_ref[...]
    conv = (cw[0:1] * pltpu.roll(z, 1, 0)[HALO:HALO + t] + cw[1:2] * z[HALO:HALO + t]
            + cw[2:3] * pltpu.roll(z, rows_ext - 1, 0)[HALO:HALO + t])
    pb = jnp.dot(n, win_ref[:, 0:D_CONV], preferred_element_type=F32)
    ymix_scr[:, 0:D_CONV] = _rms(pb * conv).astype(BF16)

    first_head = lax.broadcasted_iota(jnp.int32, (CHUNK, 128), 1) < (D_GMLP // N_GMLP_HEADS)
    chunks = []
    n_chunks = t // CHUNK
    for c in range(n_chunks):
        if c in (1, 4):
            pieces.run(1)
        vck = vln[c * CHUNK:(c + 1) * CHUNK]
        pairs = []
        for hp in range(N_GMLP_HEADS // 2):
            r = jnp.dot(ws_ref[hp], vck[:, hp * 128:(hp + 1) * 128], preferred_element_type=F32)
            pairs.append(jnp.where(first_head, r[:CHUNK], r[CHUNK:]))
        chunks.append(jnp.concatenate(pairs, axis=1) + bs_ref[...])
    mixed = jnp.concatenate(chunks, axis=0)
    ymix_scr[:, D_CONV:] = _rms(gu * mixed).astype(BF16)


def _mix(x, st, consts):
    t = st.t
    per_tile = t // HALO
    last_halo = st.n_tok // HALO - 1
    tile = lambda j: jnp.minimum(j, st.n_tiles - 1)
    row = pl.BlockSpec((t, D_MODEL), lambda j: (tile(j), 0))
    row_prev = pl.BlockSpec((t, D_MODEL), lambda j: (jnp.maximum(j - 1, 0), 0))
    nxt = pl.BlockSpec((HALO, D_MODEL),
                       lambda j: (jnp.minimum((tile(j) + 1) * per_tile, last_halo), 0))
    return pl.pallas_call(
        functools.partial(_mix_kernel, st=st),
        grid=(st.n_tiles + 1,),
        in_specs=[row, nxt, row_prev] + [_resident(a.shape) for a in consts],
        out_specs=row_prev,
        out_shape=jax.ShapeDtypeStruct(x.shape, F32),
        scratch_shapes=[pltpu.VMEM((t + 2 * HALO, D_MODEL), BF16)] + _slot_scratch(t),
        compiler_params=_params(),
        name="mix",
    )(x, x, x, *consts)


def _kv_kernel(mem_a_ref, mem_b_ref, wkv_ref, kt_ref, v_ref, *, seqs_a):
    mem = jnp.where(pl.program_id(0) < seqs_a, mem_a_ref[...], mem_b_ref[...])
    n = _rms(mem).astype(BF16)
    kv = jnp.dot(n, wkv_ref[...], preferred_element_type=F32)
    for h in range(N_XHEADS):
        kt_ref[h] = kv[:, h * XHEAD_DIM:(h + 1) * XHEAD_DIM].T.astype(BF16)
    v_ref[...] = kv[:, D_MODEL:].astype(BF16)


def _kv(mem_a, mem_b, wkv):
    ba = mem_a.shape[0]
    b = ba + mem_b.shape[0]
    return pl.pallas_call(
        functools.partial(_kv_kernel, seqs_a=ba),
        grid=(b,),
        in_specs=[pl.BlockSpec((None, N_MEM, D_MODEL), lambda bi: (jnp.minimum(bi, ba - 1), 0, 0)),
                  pl.BlockSpec((None, N_MEM, D_MODEL), lambda bi: (jnp.maximum(bi - ba, 0), 0, 0)),
                  _resident(wkv.shape)],
        out_specs=[pl.BlockSpec((None, N_XHEADS, XHEAD_DIM, N_MEM), lambda bi: (bi, 0, 0, 0)),
                   pl.BlockSpec((None, N_MEM, D_MODEL), lambda bi: (bi, 0, 0))],
        out_shape=[jax.ShapeDtypeStruct((b, N_XHEADS, XHEAD_DIM, N_MEM), BF16),
                   jax.ShapeDtypeStruct((b, N_MEM, D_MODEL), BF16)],
        compiler_params=_params(),
        name="kv_proj",
    )(mem_a, mem_b, wkv)


def _xattn_kernel(x_ref, xprev_ref, wq_ref, kt_ref, v_ref, wo_ref, o_ref, y0, y1, *, n_tiles):
    _two_phase(pl.program_id(0), n_tiles,
               functools.partial(_attend, x_ref, wq_ref, kt_ref, v_ref),
               functools.partial(_project_out_pieces, xprev_ref, wo_ref, o_ref), (y0, y1))


def _attend(x_ref, wq_ref, kt_ref, v_ref, slot, pieces):
    o_scr = slot
    x = x_ref[...]
    n = _rms(x).astype(BF16)
    pieces.run(1)
    half = n.shape[0] // 2
    q = jnp.concatenate([jnp.dot(n[r * half:(r + 1) * half], wq_ref[...],
                                 preferred_element_type=F32) for r in range(2)], axis=0)
    q = (q * (XHEAD_DIM ** -0.5)).astype(BF16)

    def scores(h):
        return jnp.dot(q[:, h * XHEAD_DIM:(h + 1) * XHEAD_DIM], kt_ref[h],
                       preferred_element_type=F32)

    s_next = scores(0)
    for h in range(N_XHEADS):
        cols = slice(h * XHEAD_DIM, (h + 1) * XHEAD_DIM)
        s = s_next
        if h + 1 < N_XHEADS:
            s_next = scores(h + 1)
        pieces.run(1 if h < 3 else 0)
        p = jnp.exp(s - jnp.max(s, axis=-1, keepdims=True))
        attn = (p * (1.0 / jnp.sum(p, axis=-1, keepdims=True))).astype(BF16)
        o_scr[:, cols] = jnp.dot(attn, v_ref[:, cols], preferred_element_type=F32).astype(BF16)


def _xattn(x, st, wq, kt, v, wo):
    t = st.t
    tile = lambda j: jnp.minimum(j, st.n_tiles - 1)
    seq = lambda j: st.seq_pos(tile(j))[0]
    row_prev = pl.BlockSpec((t, D_MODEL), lambda j: (jnp.maximum(j - 1, 0), 0))
    return pl.pallas_call(
        functools.partial(_xattn_kernel, n_tiles=st.n_tiles),
        grid=(st.n_tiles + 1,),
        in_specs=[pl.BlockSpec((t, D_MODEL), lambda j: (tile(j), 0)), row_prev,
                  _resident(wq.shape),
                  pl.BlockSpec((None, N_XHEADS, XHEAD_DIM, N_MEM), lambda j: (seq(j), 0, 0, 0)),
                  pl.BlockSpec((None, N_MEM, D_MODEL), lambda j: (seq(j), 0, 0)),
                  _resident(wo.shape)],
        out_specs=row_prev,
        out_shape=jax.ShapeDtypeStruct(x.shape, F32),
        scratch_shapes=_slot_scratch(t),
        compiler_params=_params(),
        name="xattn",
    )(x, x, wq, kt, v, wo)


def kernel(x_prompt, x_sample, mem_prompt, mem_sample, norm_ffn1, ffn1_w_gu, ffn1_w_down, norm_mix, w_in, conv_w, gmlp_w_s, gmlp_b_s, gmlp_ln_g, gmlp_ln_b, norm_conv_out, norm_gmlp_out, w_out, norm_mem, norm_cross, xattn_w_q, xattn_w_kv, xattn_w_o, norm_ffn2, ffn2_w_gu, ffn2_w_down, norm_final):
    assert ffn1_w_gu.shape[0] == 1, "single layer"
    row = lambda a: a.reshape(1, -1).astype(F32)
    bf = lambda a: a[0].astype(BF16)
    ws = gmlp_w_s[0].reshape(N_GMLP_HEADS // 2, 2 * CHUNK, CHUNK).astype(BF16)
    bs = jnp.repeat(gmlp_b_s[0].T, D_GMLP // N_GMLP_HEADS, axis=1).astype(F32)
    g_fin = row(norm_final)
    ffn1 = (_fold_gain(norm_ffn1[0], ffn1_w_gu[0]), bf(ffn1_w_down), g_fin)
    ffn2 = (_fold_gain(norm_ffn2[0], ffn2_w_gu[0]), bf(ffn2_w_down), g_fin)
    g_branches = jnp.concatenate([norm_conv_out[0], norm_gmlp_out[0]])
    mix = (_fold_gain(norm_mix[0], w_in[0]), conv_w[0].astype(F32), ws, bs, row(gmlp_ln_g[0]),
           row(gmlp_ln_b[0]), _fold_gain(g_branches, w_out[0]))

    x1 = _ffn_first(x_prompt, x_sample, ffn1)
    x2 = _mix(x1, _Stream(x_prompt.shape, x_sample.shape, MIX_TILE), mix)
    kt, v = _kv(mem_prompt, mem_sample, _fold_gain(norm_mem[0], xattn_w_kv[0]))
    x3 = _xattn(x2, _Stream(x_prompt.shape, x_sample.shape, XATTN_TILE),
                _fold_gain(norm_cross[0], xattn_w_q[0]), kt, v, bf(xattn_w_o))
    st = _Stream(x_prompt.shape, x_sample.shape, FFN_TILE)
    y_prompt = _ffn_last(x3, 0, st.tiles_a, ffn2).reshape(x_prompt.shape)
    y_sample = _ffn_last(x3, st.tiles_a, st.tiles_b, ffn2).reshape(x_sample.shape)
    return y_prompt, y_sample
```
